```python
import math
import jax, jax.numpy as jnp
from jax import lax
import numpy as np

D_MODEL = 4096
BATCH = 4
SEQ = 4096
DEPTH = 2

CHUNK = 64
Q_BLOCK = 128
D_MIX = D_MODEL
ATT_WIDTH = D_MIX // 2
ATT_HEAD_DIM = 128
ATT_HEADS = ATT_WIDTH // (2 * ATT_HEAD_DIM)
SSD_WIDTH = D_MIX - ATT_WIDTH
SSD_HEAD_DIM = 64
SSD_HEADS = SSD_WIDTH // SSD_HEAD_DIM
SSD_GROUPS = 8
SSD_STATE = 128
SSD_CONV = 4
SSD_CONV_DIM = SSD_WIDTH + 2 * SSD_GROUPS * SSD_STATE
D_FF = 11008
FFN_CONV = 3
LN_EPS = 1e-5
RMS_EPS = 1e-5
N_IN = 3 * ATT_WIDTH + SSD_WIDTH + SSD_CONV_DIM + SSD_HEADS
IN_SPLITS = [ATT_WIDTH, 2 * ATT_WIDTH, 3 * ATT_WIDTH,
             3 * ATT_WIDTH + SSD_WIDTH,
             3 * ATT_WIDTH + SSD_WIDTH + SSD_CONV_DIM]

kernel_name = 'chunk_causal_hybrid_diffattn_ssd'


def layer_norm(x, g, b):
    xf = x.astype(jnp.float32)
    mu = jnp.mean(xf, axis=-1, keepdims=True)
    var = jnp.mean(jnp.square(xf - mu), axis=-1, keepdims=True)
    y = (xf - mu) * lax.rsqrt(var + LN_EPS)
    return (y * g.astype(jnp.float32) + b.astype(jnp.float32)).astype(x.dtype)


def rms_norm(x, w):
    xf = x.astype(jnp.float32)
    y = xf * lax.rsqrt(jnp.mean(jnp.square(xf), axis=-1, keepdims=True) + RMS_EPS)
    return (y * w.astype(jnp.float32)).astype(x.dtype)


def causal_depthwise_conv(x, w, b):
    k_width = w.shape[0]
    seq = x.shape[1]
    xp = jnp.pad(x, ((0, 0), (k_width - 1, 0), (0, 0)))
    y = b + w[0] * xp[:, 0:seq]
    for j in range(1, k_width):
        y = y + w[j] * xp[:, j:j + seq]
    return y


def alibi_slopes():
    return jnp.asarray(np.array([2.0 ** (-8.0 * (h + 1) / ATT_HEADS)
                                 for h in range(ATT_HEADS)], dtype=np.float32))


def diff_attention(q, k, v, lam_params, norm_w, layer_idx):
    b, seq, _ = q.shape
    q = q.reshape(b, seq, ATT_HEADS, 2, ATT_HEAD_DIM)
    k = k.reshape(b, seq, ATT_HEADS, 2, ATT_HEAD_DIM)
    v = v.reshape(b, seq, ATT_HEADS, 2 * ATT_HEAD_DIM)
    lam_init = 0.8 - 0.6 * math.exp(-0.3 * layer_idx)
    lp = lam_params.astype(jnp.float32)
    lam = jnp.exp(jnp.sum(lp[0] * lp[1])) - jnp.exp(jnp.sum(lp[2] * lp[3])) + lam_init
    slopes = alibi_slopes()
    pos = jnp.arange(seq)
    scale = ATT_HEAD_DIM ** -0.5
    outs = []
    for i in range(seq // Q_BLOCK):
        q0 = i * Q_BLOCK
        k_end = q0 + Q_BLOCK
        qb = q[:, q0:k_end].astype(jnp.float32) * scale
        kb = k[:, :k_end].astype(jnp.float32)
        s = jnp.einsum('bqhjd,bkhjd->bhjqk', qb, kb)
        tq = pos[q0:k_end]
        tk = pos[:k_end]
        dist = jnp.abs(tq[:, None] - tk[None, :]).astype(jnp.float32)
        allowed = (tk // CHUNK)[None, :] <= (tq // CHUNK)[:, None]
        bias = jnp.where(allowed[None], -slopes[:, None, None] * dist[None], -jnp.inf)
        p = jax.nn.softmax(s + bias[None, :, None], axis=-1)
        a = p[:, :, 0] - lam * p[:, :, 1]
        outs.append(jnp.einsum('bhqk,bkhe->bqhe', a.astype(v.dtype), v[:, :k_end]))
    o = jnp.concatenate(outs, axis=1)
    o = rms_norm(o, norm_w) * (1.0 - lam_init)
    return o.reshape(b, seq, ATT_WIDTH)


def ssd_chunked_scan(xdt, da, bm, cm):
    b, seq, n_heads, p_dim = xdt.shape
    nc = seq // CHUNK
    r = n_heads // SSD_GROUPS
    x = xdt.reshape(b, nc, CHUNK, SSD_GROUPS, r, p_dim)
    bm = bm.reshape(b, nc, CHUNK, SSD_GROUPS, SSD_STATE)
    cm = cm.reshape(b, nc, CHUNK, SSD_GROUPS, SSD_STATE)
    da = da.reshape(b, nc, CHUNK, SSD_GROUPS, r).transpose(0, 3, 4, 1, 2)
    a_cs = jnp.cumsum(da, axis=-1)
    li = jnp.arange(CHUNK)
    causal = li[:, None] >= li[None, :]
    seg = a_cs[..., :, None] - a_cs[..., None, :]
    decay_in = jnp.exp(jnp.where(causal, seg, -jnp.inf))
    cb = jnp.einsum('bclgn,bcsgn->bgcls', cm, bm)
    y_diag = jnp.einsum('bgrcls,bcsgrp->bclgrp', cb[:, :, None] * decay_in, x)
    decay_states = jnp.exp(a_cs[..., -1:] - a_cs).transpose(0, 3, 4, 1, 2)
    states = jnp.einsum('bcsgn,bcsgrp->bcgrpn', bm, x * decay_states[..., None])
    chunk_decay = jnp.exp(a_cs[..., -1])

    def step(h, inp):
        st, dec = inp
        return dec[..., None, None] * h + st, h

    h0 = jnp.zeros((b, SSD_GROUPS, r, p_dim, SSD_STATE), dtype=jnp.float32)
    _, prev = lax.scan(step, h0, (states.transpose(1, 0, 2, 3, 4, 5),
                                  chunk_decay.transpose(3, 0, 1, 2)))
    prev = prev.transpose(1, 0, 2, 3, 4, 5)
    state_decay = jnp.exp(a_cs).transpose(0, 3, 4, 1, 2)
    y_off = jnp.einsum('bclgn,bcgrpn->bclgrp', cm, prev) * state_decay[..., None]
    return (y_diag + y_off).reshape(b, seq, n_heads, p_dim)


def ssd_mixer(z, xbc, dt_raw, conv_w, conv_b, dt_bias, a_log, d_skip, norm_w):
    b, seq, _ = z.shape
    xbc = jax.nn.silu(causal_depthwise_conv(xbc, conv_w, conv_b))
    xs, bm, cm = jnp.split(xbc, [SSD_WIDTH, SSD_WIDTH + SSD_GROUPS * SSD_STATE], axis=-1)
    xs = xs.reshape(b, seq, SSD_HEADS, SSD_HEAD_DIM).astype(jnp.float32)
    bm = bm.reshape(b, seq, SSD_GROUPS, SSD_STATE).astype(jnp.float32)
    cm = cm.reshape(b, seq, SSD_GROUPS, SSD_STATE).astype(jnp.float32)
    dt = jax.nn.softplus(dt_raw.astype(jnp.float32) + dt_bias.astype(jnp.float32))
    a = -jnp.exp(a_log.astype(jnp.float32))
    y = ssd_chunked_scan(xs * dt[..., None], dt * a, bm, cm)
    y = y + d_skip.astype(jnp.float32)[:, None] * xs
    y = y.reshape(b, seq, SSD_WIDTH) * jax.nn.silu(z.astype(jnp.float32))
    y = rms_norm(y.reshape(b, seq, SSD_GROUPS, SSD_WIDTH // SSD_GROUPS),
                 norm_w.reshape(SSD_GROUPS, SSD_WIDTH // SSD_GROUPS))
    return y.reshape(b, seq, SSD_WIDTH).astype(z.dtype)


def conv_glu_ffn(h, w_gate, w_up, conv_w, conv_b, w_down):
    g = causal_depthwise_conv(h @ w_gate, conv_w, conv_b)
    return (jax.nn.silu(g) * (h @ w_up)) @ w_down


def setup_inputs(seed: int = 0) -> dict:
    key = jax.random.key(seed)
    ks = jax.random.split(key, 24)
    f32 = jnp.float32
    beta = (8.0 * DEPTH) ** -0.25
    nrm = lambda k, shape, s: jax.random.normal(k, shape, f32) * s
    dt0 = jnp.exp(jax.random.uniform(ks[9], (DEPTH, SSD_HEADS), f32,
                                     math.log(1e-3), math.log(1e-1)))
    return {
        'x': nrm(ks[0], (BATCH, SEQ, D_MODEL), 1.0),
        'c': nrm(ks[1], (BATCH, D_MODEL), 1.0),
        'w_mod': nrm(ks[2], (DEPTH, D_MODEL, 6 * D_MODEL), D_MODEL ** -0.5),
        'b_mod': nrm(ks[3], (DEPTH, 6 * D_MODEL), 0.02),
        'w_in': nrm(ks[4], (DEPTH, D_MODEL, N_IN), D_MODEL ** -0.5),
        'diff_lambda': nrm(ks[5], (DEPTH, 4, ATT_HEAD_DIM), 0.1),
        'diff_norm_w': 1.0 + nrm(ks[6], (DEPTH, 2 * ATT_HEAD_DIM), 0.02),
        'ssd_conv_w': nrm(ks[7], (DEPTH, SSD_CONV, SSD_CONV_DIM), SSD_CONV ** -0.5),
        'ssd_conv_b': nrm(ks[8], (DEPTH, SSD_CONV_DIM), 0.02),
        'ssd_dt_bias': dt0 + jnp.log(-jnp.expm1(-dt0)),
        'ssd_a_log': jnp.log(jax.random.uniform(ks[10], (DEPTH, SSD_HEADS), f32, 1.0, 16.0)),
        'ssd_d': 1.0 + nrm(ks[11], (DEPTH, SSD_HEADS), 0.02),
        'ssd_norm_w': 1.0 + nrm(ks[12], (DEPTH, SSD_WIDTH), 0.02),
        'w_out': nrm(ks[13], (DEPTH, D_MIX, D_MODEL), beta * D_MIX ** -0.5),
        'ln1_g': 1.0 + nrm(ks[14], (DEPTH, D_MODEL), 0.02),
        'ln1_b': nrm(ks[15], (DEPTH, D_MODEL), 0.02),
        'w_gate': nrm(ks[16], (DEPTH, D_MODEL, D_FF), D_MODEL ** -0.5),
        'w_up': nrm(ks[17], (DEPTH, D_MODEL, D_FF), D_MODEL ** -0.5),
        'ffn_conv_w': nrm(ks[18], (DEPTH, FFN_CONV, D_FF), FFN_CONV ** -0.5),
        'ffn_conv_b': nrm(ks[19], (DEPTH, D_FF), 0.02),
        'w_down': nrm(ks[20], (DEPTH, D_FF, D_MODEL), beta * D_FF ** -0.5),
        'ln2_g': 1.0 + nrm(ks[21], (DEPTH, D_MODEL), 0.02),
        'ln2_b': nrm(ks[22], (DEPTH, D_MODEL), 0.02),
    }


def reference(x, c, w_mod, b_mod, w_in, diff_lambda, diff_norm_w, ssd_conv_w, ssd_conv_b,
              ssd_dt_bias, ssd_a_log, ssd_d, ssd_norm_w, w_out, ln1_g, ln1_b,
              w_gate, w_up, ffn_conv_w, ffn_conv_b, w_down, ln2_g, ln2_b):
    alpha = (2.0 * DEPTH) ** 0.25
    c_act = jax.nn.silu(c)
    for l in range(DEPTH):
        mod = (c_act @ w_mod[l] + b_mod[l])[:, None, :]
        shift1, scale1, gate1, shift2, scale2, gate2 = jnp.split(mod, 6, axis=-1)
        h = x * (1.0 + scale1) + shift1
        proj = h @ w_in[l]
        q, k, v, z, xbc, dt_raw = jnp.split(proj, IN_SPLITS, axis=-1)
        y_att = diff_attention(q, k, v, diff_lambda[l], diff_norm_w[l], l)
        y_ssd = ssd_mixer(z, xbc, dt_raw, ssd_conv_w[l], ssd_conv_b[l], ssd_dt_bias[l],
                          ssd_a_log[l], ssd_d[l], ssd_norm_w[l])
        m = jnp.concatenate([y_att, y_ssd], axis=-1) @ w_out[l]
        x = layer_norm(alpha * x + gate1 * m, ln1_g[l], ln1_b[l])
        h = x * (1.0 + scale2) + shift2
        f = conv_glu_ffn(h, w_gate[l], w_up[l], ffn_conv_w[l], ffn_conv_b[l], w_down[l])
        x = layer_norm(alpha * x + gate2 * f, ln2_g[l], ln2_b[l])
    return x
```

```python
import functools

import numpy as np
import jax
import jax.numpy as jnp
from jax import lax
from jax.experimental import pallas as pl
from jax.experimental.pallas import tpu as pltpu

F32 = jnp.float32
BF16 = jnp.bfloat16

CHUNK = 64
ATT_HEAD_DIM = 128
ATT_VALUE_DIM = 2 * ATT_HEAD_DIM
SSD_HEAD_DIM = 64
SSD_GROUPS = 8
SSD_STATE = 128
SSD_CONV = 4
FFN_CONV = 3
LN_EPS = 1e-5
RMS_EPS = 1e-5

V7X_LANES = 128
V7X_SUBLANES = 8
V7X_BF16_SUBLANES = 16
V7X_VMEM_BYTES = 64 * 1024 * 1024
VMEM_REQUEST_CAP = V7X_VMEM_BYTES - 8 * 1024 * 1024

MASK_VALUE = -1e30


def _nbytes(shape, dtype):
    return int(np.prod(shape)) * jnp.dtype(dtype).itemsize


def _params(semantics, blocks, extra_bytes=0):
    need = 2 * sum(_nbytes(s, d) for s, d in blocks) + extra_bytes + (4 << 20)
    return pltpu.CompilerParams(dimension_semantics=semantics,
                                vmem_limit_bytes=int(min(max(need, 16 << 20), VMEM_REQUEST_CAP)))


def _silu(v):
    return v * jax.nn.sigmoid(v)


def _softplus(v):
    return jnp.maximum(v, 0.0) + jnp.log1p(jnp.exp(-jnp.abs(v)))


def _mod_kernel(c_ref, w_ref, b_ref, o_ref):
    @pl.when(pl.program_id(2) == 0)
    def _():
        o_ref[...] = jnp.broadcast_to(b_ref[...], o_ref.shape)

    c_act = _silu(c_ref[...]).astype(BF16)
    o_ref[...] += jnp.dot(c_act, w_ref[...].astype(BF16), preferred_element_type=F32)


def _modulation(c, w_mod, b_mod):
    depth, d, n = w_mod.shape
    rows = V7X_SUBLANES
    c_pad = jnp.zeros((rows, d), F32).at[:c.shape[0]].set(c)
    tk, tn = min(1024, d), min(2048, n)
    blocks = [((rows, tk), F32), ((tk, tn), F32), ((1, tn), F32), ((rows, tn), F32)]
    return pl.pallas_call(
        _mod_kernel,
        grid=(depth, n // tn, d // tk),
        in_specs=[pl.BlockSpec((rows, tk), lambda l, j, k: (0, k)),
                  pl.BlockSpec((None, tk, tn), lambda l, j, k: (l, k, j)),
                  pl.BlockSpec((None, 1, tn), lambda l, j, k: (l, 0, j))],
        out_specs=pl.BlockSpec((None, rows, tn), lambda l, j, k: (l, 0, j)),
        out_shape=jax.ShapeDtypeStruct((depth, rows, n), F32),
        name="adaln_mod",
        compiler_params=_params(("parallel", "parallel", "arbitrary"), blocks,
                                extra_bytes=_nbytes((tk, tn), BF16)),
    )(c_pad, w_mod, b_mod.reshape(depth, 1, n))


def _modulate_kernel(x_ref, scale_ref, shift_ref, h_ref):
    h_ref[...] = (x_ref[...] * (1.0 + scale_ref[...]) + shift_ref[...]).astype(h_ref.dtype)


def _modulate(x2, scale, shift, seq):
    m, d = x2.shape
    batch = m // seq
    ts = min(256, seq)
    n_s = seq // ts
    vec = pl.BlockSpec((None, 1, d), lambda b, i: (b, 0, 0))
    blocks = [((ts, d), F32), ((ts, d), BF16)]
    return pl.pallas_call(
        _modulate_kernel,
        grid=(batch, n_s),
        in_specs=[pl.BlockSpec((ts, d), lambda b, i: (b * n_s + i, 0)), vec, vec],
        out_specs=pl.BlockSpec((ts, d), lambda b, i: (b * n_s + i, 0)),
        out_shape=jax.ShapeDtypeStruct((m, d), BF16),
        name="modulate",
        compiler_params=_params(("parallel", "parallel"), blocks),
    )(x2, scale, shift)


def _matmul_kernel(a_ref, w_ref, o_ref):
    o_ref[...] = jnp.dot(a_ref[...], w_ref[...], preferred_element_type=F32).astype(o_ref.dtype)


def _matmul(a, w, *, tm, tn, out_dtype, name):
    m, k = a.shape
    n = w.shape[1]
    tm, tn = min(tm, m), min(tn, n)
    blocks = [((tm, k), a.dtype), ((k, tn), w.dtype), ((tm, tn), out_dtype)]
    return pl.pallas_call(
        _matmul_kernel,
        grid=(m // tm, n // tn),
        in_specs=[pl.BlockSpec((tm, k), lambda i, j: (i, 0)),
                  pl.BlockSpec((k, tn), lambda i, j: (0, j))],
        out_specs=pl.BlockSpec((tm, tn), lambda i, j: (i, j)),
        out_shape=jax.ShapeDtypeStruct((m, n), out_dtype),
        name=name,
        compiler_params=_params(("parallel", "arbitrary"), blocks, extra_bytes=_nbytes((tm, tn), F32)),
    )(a, w)


def _matmul_cat_kernel(a0_ref, a1_ref, w0_ref, w1_ref, o_ref):
    acc = jnp.dot(a0_ref[...], w0_ref[...], preferred_element_type=F32)
    acc += jnp.dot(a1_ref[...], w1_ref[...], preferred_element_type=F32)
    o_ref[...] = acc.astype(o_ref.dtype)


def _matmul_cat(a0, a1, w, *, tm, tn, out_dtype):
    m, k0 = a0.shape
    assert a1.shape == (m, k0) and w.shape[0] == 2 * k0
    n = w.shape[1]
    tm, tn = min(tm, m), min(tn, n)
    blocks = [((tm, 2 * k0), a0.dtype), ((2 * k0, tn), w.dtype), ((tm, tn), out_dtype)]
    return pl.pallas_call(
        _matmul_cat_kernel,
        grid=(m // tm, n // tn),
        in_specs=[pl.BlockSpec((tm, k0), lambda i, j: (i, 0)),
                  pl.BlockSpec((tm, k0), lambda i, j: (i, 0)),
                  pl.BlockSpec((k0, tn), lambda i, j: (0, j)),
                  pl.BlockSpec((k0, tn), lambda i, j: (1, j))],
        out_specs=pl.BlockSpec((tm, tn), lambda i, j: (i, j)),
        out_shape=jax.ShapeDtypeStruct((m, n), out_dtype),
        name="out_proj",
        compiler_params=_params(("parallel", "arbitrary"), blocks, extra_bytes=_nbytes((tm, tn), F32)),
    )(a0, a1, w, w)


def _attn_kernel(slopes_ref, q_ref, k_ref, v_ref, lam_ref, nw_ref, o_ref, m_ref, l_ref, acc_ref,
                 *, tile, lam_init):
    h = pl.program_id(1)
    i = pl.program_id(2)
    slope = slopes_ref[h]
    dh = ATT_HEAD_DIM
    q = q_ref[...]
    qs = [(q[:, mp * dh:(mp + 1) * dh].astype(F32) * (dh ** -0.5)).astype(BF16) for mp in range(2)]
    row = lax.broadcasted_iota(jnp.int32, (tile, tile), 0)
    col = lax.broadcasted_iota(jnp.int32, (tile, tile), 1)
    rel = (row - col).astype(F32)

    def scores(k0, mp):
        kt = k_ref[pl.ds(k0, tile), mp * dh:(mp + 1) * dh]
        return lax.dot_general(qs[mp], kt, (((1,), (1,)), ((), ())), preferred_element_type=F32)

    allowed = (col // CHUNK) <= (row // CHUNK)
    bias_diag = jnp.where(allowed, -slope * jnp.abs(rel), MASK_VALUE)
    k_diag = pl.multiple_of(i * tile, tile)
    v_diag = v_ref[pl.ds(k_diag, tile), :]
    for mp in range(2):
        s = scores(k_diag, mp) + bias_diag
        mx = jnp.max(s, axis=-1, keepdims=True)
        p = jnp.exp(s - mx)
        m_ref[mp] = mx
        l_ref[mp] = jnp.sum(p, axis=-1, keepdims=True)
        acc_ref[mp] = jnp.dot(p.astype(BF16), v_diag, preferred_element_type=F32)

    bias_off = -slope * rel

    def body(j, carry):
        k0 = pl.multiple_of(j * tile, tile)
        tile_bias = -slope * ((i - j) * tile).astype(F32)
        v_j = v_ref[pl.ds(k0, tile), :]
        for mp in range(2):
            s = scores(k0, mp) + bias_off
            m_old = m_ref[mp]
            m_new = jnp.maximum(m_old, jnp.max(s, axis=-1, keepdims=True) + tile_bias)
            alpha = jnp.exp(m_old - m_new)
            p = jnp.exp(s - (m_new - tile_bias))
            l_ref[mp] = alpha * l_ref[mp] + jnp.sum(p, axis=-1, keepdims=True)
            acc_ref[mp] = alpha * acc_ref[mp] + jnp.dot(p.astype(BF16), v_j, preferred_element_type=F32)
            m_ref[mp] = m_new
        return carry

    lax.fori_loop(0, i, body, 0)

    lp = lam_ref[...]
    lam = (jnp.exp(jnp.sum(lp[0:1] * lp[1:2], axis=-1, keepdims=True))
           - jnp.exp(jnp.sum(lp[2:3] * lp[3:4], axis=-1, keepdims=True)) + lam_init)
    o = acc_ref[0] * (1.0 / l_ref[0]) - lam * (acc_ref[1] * (1.0 / l_ref[1]))
    ms = jnp.mean(o * o, axis=-1, keepdims=True)
    y = o * lax.rsqrt(ms + RMS_EPS) * nw_ref[...] * (1.0 - lam_init)
    o_ref[...] = y.astype(o_ref.dtype)


def _attention(proj, lam_params, norm_w, *, seq, n_heads, lam_init):
    m = proj.shape[0]
    batch = m // seq
    tile = min(512, seq)
    n_q = seq // tile
    dv = ATT_VALUE_DIM
    slopes = jnp.asarray(np.array([2.0 ** (-8.0 * (hh + 1) / n_heads) for hh in range(n_heads)],
                                  dtype=np.float32))
    blocks = [((tile, dv), BF16), ((seq, dv), BF16), ((seq, dv), BF16), ((tile, dv), BF16)]
    scratch_bytes = (2 * _nbytes((2, tile, V7X_LANES), F32) + _nbytes((2, tile, dv), F32)
                     + 6 * _nbytes((tile, tile), F32))
    grid_spec = pltpu.PrefetchScalarGridSpec(
        num_scalar_prefetch=1,
        grid=(batch, n_heads, n_q),
        in_specs=[pl.BlockSpec((tile, dv), lambda b, h, i, s: (b * n_q + i, h)),
                  pl.BlockSpec((seq, dv), lambda b, h, i, s: (b, n_heads + h)),
                  pl.BlockSpec((seq, dv), lambda b, h, i, s: (b, 2 * n_heads + h)),
                  pl.BlockSpec((4, ATT_HEAD_DIM), lambda b, h, i, s: (0, 0)),
                  pl.BlockSpec((1, dv), lambda b, h, i, s: (0, 0))],
        out_specs=pl.BlockSpec((tile, dv), lambda b, h, i, s: (b * n_q + i, h)),
        scratch_shapes=[pltpu.VMEM((2, tile, 1), F32), pltpu.VMEM((2, tile, 1), F32),
                        pltpu.VMEM((2, tile, dv), F32)],
    )
    return pl.pallas_call(
        functools.partial(_attn_kernel, tile=tile, lam_init=lam_init),
        grid_spec=grid_spec,
        out_shape=jax.ShapeDtypeStruct((m, n_heads * dv), BF16),
        name="diff_attention",
        compiler_params=_params(("parallel", "parallel", "arbitrary"), blocks, extra_bytes=scratch_bytes),
    )(slopes, proj, proj, proj, lam_params, norm_w.reshape(1, dv))


def _ssd_kernel(z_ref, xbc_ref, halo_ref, dt_ref, dtt_ref, cw_ref, cb_ref, dtb_l_ref, alog_l_ref,
                dtb_s_ref, alog_s_ref, d_ref, nw_ref, o_ref, state_ref, *, tile, width):
    t = pl.program_id(1)
    gw = width // SSD_GROUPS
    hpg = gw // SSD_HEAD_DIM
    n = SSD_STATE
    b_off, c_off = width, width + SSD_GROUPS * n

    @pl.when(t == 0)
    def _():
        state_ref[...] = jnp.zeros_like(state_ref)

    row = lax.broadcasted_iota(jnp.int32, (tile, tile), 0)
    col = lax.broadcasted_iota(jnp.int32, (tile, tile), 1)
    causal = row >= col
    dt_s = _softplus(dt_ref[...] + dtb_l_ref[...])
    da_s = dt_s * (-jnp.exp(alog_l_ref[...]))
    acs_s = jnp.dot(causal.astype(F32), da_s, precision=lax.Precision.HIGHEST,
                    preferred_element_type=F32)
    dt_l = _softplus(dtt_ref[...] + dtb_s_ref[...])
    da_l = dt_l * (-jnp.exp(alog_s_ref[...]))
    acs_l = jnp.dot(da_l, (row <= col).astype(F32), precision=lax.Precision.HIGHEST,
                    preferred_element_type=F32)

    head_of_lane = lax.broadcasted_iota(jnp.int32, (1, gw), 1) // SSD_HEAD_DIM

    def per_head(vals):
        out = vals[hpg - 1]
        for r in range(hpg - 2, -1, -1):
            out = jnp.where(head_of_lane == r, vals[r], out)
        return out

    def conv_silu(c0, w):
        xcur = xbc_ref[:, c0:c0 + w].astype(F32)
        xprev = jnp.where(t > 0, halo_ref[:, c0:c0 + w].astype(F32), 0.0)
        halo_rows = xprev.shape[0]
        xe = jnp.concatenate([xprev, xcur], axis=0)
        y = cb_ref[:, c0:c0 + w]
        for j in range(SSD_CONV):
            off = halo_rows - (SSD_CONV - 1) + j
            y = y + cw_ref[j:j + 1, c0:c0 + w] * xe[off:off + tile]
        return _silu(y)

    for g in range(SSD_GROUPS):
        xs = conv_silu(g * gw, gw)
        bm = conv_silu(b_off + g * n, n).astype(BF16)
        cm = conv_silu(c_off + g * n, n).astype(BF16)
        heads = [g * hpg + r for r in range(hpg)]
        acs_cols = [acs_s[:, hh:hh + 1] for hh in heads]
        xdt = xs * per_head([dt_s[:, hh:hh + 1] for hh in heads])
        cb = lax.dot_general(cm, bm, (((1,), (1,)), ((), ())), preferred_element_type=F32)
        prev = state_ref[g]
        y = jnp.dot(cm, prev.astype(BF16), preferred_element_type=F32)
        y = y * per_head([jnp.exp(a) for a in acs_cols])
        for r, hh in enumerate(heads):
            seg = acs_cols[r] - acs_l[hh:hh + 1, :]
            decay = jnp.exp(jnp.where(causal, seg, MASK_VALUE))
            x_r = jnp.where(head_of_lane == r, xdt, 0.0).astype(BF16)
            y = y + jnp.dot((cb * decay).astype(BF16), x_r, preferred_element_type=F32)
        last = [a[tile - 1:tile, :] for a in acs_cols]
        to_end = per_head([jnp.exp(last[r] - acs_cols[r]) for r in range(hpg)])
        new_state = lax.dot_general(bm, (xdt * to_end).astype(BF16), (((0,), (0,)), ((), ())),
                                    preferred_element_type=F32)
        state_ref[g] = prev * per_head([jnp.exp(a) for a in last]) + new_state
        lanes = slice(g * gw, (g + 1) * gw)
        y = y + d_ref[:, lanes] * xs
        y = y * _silu(z_ref[:, lanes].astype(F32))
        y = y * lax.rsqrt(jnp.mean(y * y, axis=-1, keepdims=True) + RMS_EPS) * nw_ref[:, lanes]
        o_ref[:, lanes] = y.astype(o_ref.dtype)


def _ssd(proj, dt_raw, conv_w, conv_b, dt_bias, a_log, d_skip, norm_w, *, seq, z_col, xbc_col):
    m = proj.shape[0]
    batch = m // seq
    n_heads = dt_bias.shape[0]
    width = n_heads * SSD_HEAD_DIM
    conv_dim = conv_w.shape[1]
    tile = min(256, seq)
    n_t = seq // tile
    halo = V7X_BF16_SUBLANES
    halo_per_tile = tile // halo
    lanes = dt_raw.shape[1]

    def lane_vec(v):
        return jnp.zeros((1, lanes), F32).at[0, :n_heads].set(v)

    dtt = dt_raw[:, :n_heads].T
    const = lambda b, t: (0, 0)
    blocks = [((tile, width), BF16), ((tile, conv_dim), BF16), ((halo, conv_dim), BF16),
              ((tile, lanes), F32), ((n_heads, tile), F32), ((tile, width), BF16)]
    scratch_bytes = _nbytes((SSD_GROUPS, SSD_STATE, width // SSD_GROUPS), F32) + 12 * _nbytes((tile, tile), F32) \
        + 4 * _nbytes((tile + halo, conv_dim // 4), F32)
    return pl.pallas_call(
        functools.partial(_ssd_kernel, tile=tile, width=width),
        grid=(batch, n_t),
        in_specs=[pl.BlockSpec((tile, width), lambda b, t: (b * n_t + t, z_col)),
                  pl.BlockSpec((tile, conv_dim), lambda b, t: (b * n_t + t, xbc_col)),
                  pl.BlockSpec((halo, conv_dim),
                               lambda b, t: (jnp.maximum((b * n_t + t) * halo_per_tile - 1, 0), xbc_col)),
                  pl.BlockSpec((tile, lanes), lambda b, t: (b * n_t + t, 0)),
                  pl.BlockSpec((n_heads, tile), lambda b, t: (0, b * n_t + t)),
                  pl.BlockSpec((SSD_CONV, conv_dim), const),
                  pl.BlockSpec((1, conv_dim), const),
                  pl.BlockSpec((1, lanes), const),
                  pl.BlockSpec((1, lanes), const),
                  pl.BlockSpec((n_heads, 1), const),
                  pl.BlockSpec((n_heads, 1), const),
                  pl.BlockSpec((1, width), const),
                  pl.BlockSpec((1, width), const)],
        out_specs=pl.BlockSpec((tile, width), lambda b, t: (b * n_t + t, 0)),
        out_shape=jax.ShapeDtypeStruct((m, width), BF16),
        name="ssd_mixer",
        scratch_shapes=[pltpu.VMEM((SSD_GROUPS, SSD_STATE, width // SSD_GROUPS), F32)],
        compiler_params=_params(("parallel", "arbitrary"), blocks, extra_bytes=scratch_bytes),
    )(proj, proj, proj, dt_raw, dtt, conv_w, conv_b.reshape(1, conv_dim), lane_vec(dt_bias), lane_vec(a_log),
      dt_bias.reshape(n_heads, 1), a_log.reshape(n_heads, 1),
      jnp.repeat(d_skip, SSD_HEAD_DIM).reshape(1, width), norm_w.reshape(1, width))


def _ln_kernel(*refs, alpha, emit_next):
    if emit_next:
        x_ref, br_ref, gate_ref, g_ref, b_ref, scale_ref, shift_ref, xo_ref, h_ref = refs
    else:
        x_ref, br_ref, gate_ref, g_ref, b_ref, xo_ref = refs
    v = alpha * x_ref[...] + gate_ref[...] * br_ref[...].astype(F32)
    mu = jnp.mean(v, axis=-1, keepdims=True)
    dev = v - mu
    var = jnp.mean(dev * dev, axis=-1, keepdims=True)
    y = dev * lax.rsqrt(var + LN_EPS) * g_ref[...] + b_ref[...]
    xo_ref[...] = y
    if emit_next:
        h_ref[...] = (y * (1.0 + scale_ref[...]) + shift_ref[...]).astype(h_ref.dtype)


def _residual_ln(x2, branch, gate, ln_g, ln_b, next_scale, next_shift, *, seq, alpha):
    m, d = x2.shape
    batch = m // seq
    ts = min(256, seq)
    n_s = seq // ts
    emit_next = next_scale is not None
    rows = pl.BlockSpec((ts, d), lambda b, i: (b * n_s + i, 0))
    per_batch = pl.BlockSpec((None, 1, d), lambda b, i: (b, 0, 0))
    shared = pl.BlockSpec((1, d), lambda b, i: (0, 0))
    in_specs = [rows, rows, per_batch, shared, shared]
    args = [x2, branch, gate, ln_g.reshape(1, d), ln_b.reshape(1, d)]
    out_specs = [rows]
    out_shape = [jax.ShapeDtypeStruct((m, d), F32)]
    blocks = [((ts, d), F32), ((ts, d), branch.dtype), ((ts, d), F32)]
    if emit_next:
        in_specs += [per_batch, per_batch]
        args += [next_scale, next_shift]
        out_specs.append(rows)
        out_shape.append(jax.ShapeDtypeStruct((m, d), BF16))
        blocks.append(((ts, d), BF16))
    out = pl.pallas_call(
        functools.partial(_ln_kernel, alpha=alpha, emit_next=emit_next),
        grid=(batch, n_s),
        in_specs=in_specs,
        out_specs=out_specs,
        out_shape=out_shape,
        name="residual_ln",
        compiler_params=_params(("parallel", "parallel"), blocks, extra_bytes=2 * _nbytes((ts, d), F32)),
    )(*args)
    return (out[0], out[1]) if emit_next else (out[0], None)


def _ffn_up_kernel(h_ref, w_ref, cw_ref, cb_ref, o_ref, carry_ref, *, tm, tn, tiles_per_seq):
    i = pl.program_id(0)
    j = pl.program_id(1)
    r = jnp.dot(h_ref[...], w_ref[...], preferred_element_type=F32)
    gate, up = r[:, :tn], r[:, tn:]
    prev = jnp.where(i % tiles_per_seq == 0, 0.0, carry_ref[j])
    carry_ref[j] = gate[tm - V7X_SUBLANES:, :]
    ge = jnp.concatenate([prev, gate], axis=0)
    y = cb_ref[...]
    for tap in range(FFN_CONV):
        off = V7X_SUBLANES - (FFN_CONV - 1) + tap
        y = y + cw_ref[tap:tap + 1, :] * ge[off:off + tm]
    o_ref[...] = (_silu(y) * up).astype(o_ref.dtype)


def _ffn_up(h, w_gu, conv_w, conv_b, *, seq, tn):
    m, d = h.shape
    f = w_gu.shape[1] // 2
    tm = min(1024, seq)
    n_j = f // tn
    blocks = [((tm, d), BF16), ((d, 2 * tn), BF16), ((tm, tn), BF16)]
    return pl.pallas_call(
        functools.partial(_ffn_up_kernel, tm=tm, tn=tn, tiles_per_seq=seq // tm),
        grid=(m // tm, n_j),
        in_specs=[pl.BlockSpec((tm, d), lambda i, j: (i, 0)),
                  pl.BlockSpec((d, 2 * tn), lambda i, j: (0, j)),
                  pl.BlockSpec((FFN_CONV, tn), lambda i, j: (0, j)),
                  pl.BlockSpec((1, tn), lambda i, j: (0, j))],
        out_specs=pl.BlockSpec((tm, tn), lambda i, j: (i, j)),
        out_shape=jax.ShapeDtypeStruct((m, f), BF16),
        name="ffn_up",
        scratch_shapes=[pltpu.VMEM((n_j, V7X_SUBLANES, tn), F32)],
        compiler_params=_params(("arbitrary", "arbitrary"), blocks,
                                extra_bytes=6 * _nbytes((tm, tn), F32)),
    )(h, w_gu, conv_w, conv_b.reshape(1, f))


def _interleave_gate_up(w_gate, w_up, tn):
    d, f = w_gate.shape
    both = jnp.stack([w_gate.reshape(d, f // tn, tn), w_up.reshape(d, f // tn, tn)], axis=2)
    return both.reshape(d, 2 * f).astype(BF16)


def kernel(x, c, w_mod, b_mod, w_in, diff_lambda, diff_norm_w, ssd_conv_w, ssd_conv_b, ssd_dt_bias,
           ssd_a_log, ssd_d, ssd_norm_w, w_out, ln1_g, ln1_b, w_gate, w_up, ffn_conv_w, ffn_conv_b,
           w_down, ln2_g, ln2_b):
    batch, seq, d = x.shape
    depth = w_mod.shape[0]
    m = batch * seq
    n_ssd_heads = ssd_dt_bias.shape[1]
    ssd_width = n_ssd_heads * SSD_HEAD_DIM
    conv_dim = ssd_conv_w.shape[2]
    n_in = w_in.shape[2]
    att_width = (n_in - n_ssd_heads - conv_dim - ssd_width) // 3
    n_att_heads = att_width // ATT_VALUE_DIM
    n_main = n_in - n_ssd_heads
    assert att_width == ssd_width and conv_dim == 2 * ssd_width
    alpha = (2.0 * depth) ** 0.25
    ffn_tn = 256

    mod = _modulation(c, w_mod, b_mod)[:, :batch]
    mod = mod.reshape(depth, batch, 6, 1, d)
    shift1, scale1, gate1, shift2, scale2, gate2 = (mod[:, :, s] for s in range(6))

    x2 = x.reshape(m, d)
    h = _modulate(x2, scale1[0], shift1[0], seq)
    for l in range(depth):
        w_in_main = w_in[l, :, :n_main].astype(BF16)
        w_dt = jnp.zeros((d, V7X_LANES), BF16).at[:, :n_ssd_heads].set(w_in[l, :, n_main:].astype(BF16))
        proj = _matmul(h, w_in_main, tm=1024, tn=1024, out_dtype=BF16, name="in_proj")
        dt_raw = _matmul(h, w_dt, tm=1024, tn=V7X_LANES, out_dtype=F32, name="dt_proj")
        lam_init = 0.8 - 0.6 * float(np.exp(-0.3 * l))
        y_att = _attention(proj, diff_lambda[l], diff_norm_w[l], seq=seq, n_heads=n_att_heads,
                           lam_init=lam_init)
        y_ssd = _ssd(proj, dt_raw, ssd_conv_w[l], ssd_conv_b[l], ssd_dt_bias[l], ssd_a_log[l], ssd_d[l],
                     ssd_norm_w[l], seq=seq, z_col=3 * att_width // ssd_width,
                     xbc_col=(3 * att_width + ssd_width) // conv_dim)
        mix = _matmul_cat(y_att, y_ssd, w_out[l].astype(BF16), tm=1024, tn=1024, out_dtype=F32)
        x2, h = _residual_ln(x2, mix, gate1[l], ln1_g[l], ln1_b[l], scale2[l], shift2[l],
                             seq=seq, alpha=alpha)
        act = _ffn_up(h, _interleave_gate_up(w_gate[l], w_up[l], ffn_tn), ffn_conv_w[l], ffn_conv_b[l],
                      seq=seq, tn=ffn_tn)
        ffn = _matmul(act, w_down[l].astype(BF16), tm=512, tn=512, out_dtype=F32, name="ffn_down")
        last = l == depth - 1
        x2, h = _residual_ln(x2, ffn, gate2[l], ln2_g[l], ln2_b[l],
                             None if last else scale1[l + 1], None if last else shift1[l + 1],
                             seq=seq, alpha=alpha)
    return x2.reshape(batch, seq, d)
```

```python
import functools

import numpy as np
import jax
import jax.numpy as jnp
from jax import lax
from jax.experimental import pallas as pl
from jax.experimental.pallas import tpu as pltpu

F32 = jnp.float32
BF16 = jnp.bfloat16

CHUNK = 64
ATT_HEAD_DIM = 128
ATT_VALUE_DIM = 2 * ATT_HEAD_DIM
SSD_HEAD_DIM = 64
SSD_GROUPS = 8
SSD_STATE = 128
SSD_CONV = 4
FFN_CONV = 3
LN_EPS = 1e-5
RMS_EPS = 1e-5

V7X_LANES = 128
V7X_SUBLANES = 8
V7X_BF16_SUBLANES = 16
V7X_VMEM_BYTES = 64 * 1024 * 1024
VMEM_REQUEST_CAP = V7X_VMEM_BYTES - 8 * 1024 * 1024

MASK_VALUE = -1e30
LOG2_E = 1.4426950408889634
ATT_ROW_BLOCK = 128
ATT_DIAG_ROW_BLOCK = 256
DIAG_MASK_PER_SLOPE = -1e32


def _nbytes(shape, dtype):
    return int(np.prod(shape)) * jnp.dtype(dtype).itemsize


def _params(semantics, blocks, extra_bytes=0):
    need = 2 * sum(_nbytes(s, d) for s, d in blocks) + extra_bytes + (4 << 20)
    return pltpu.CompilerParams(dimension_semantics=semantics,
                                vmem_limit_bytes=int(min(max(need, 16 << 20), VMEM_REQUEST_CAP)))


def _silu(v):
    return v * (0.5 * jnp.tanh(0.5 * v) + 0.5)


def _softplus(v):
    return jnp.maximum(v, 0.0) + jnp.log1p(jnp.exp(-jnp.abs(v)))


def _mod_kernel(c_ref, w_ref, b_ref, o_ref):
    @pl.when(pl.program_id(2) == 0)
    def _():
        o_ref[...] = jnp.broadcast_to(b_ref[...], o_ref.shape)

    c_act = _silu(c_ref[...]).astype(BF16)
    o_ref[...] += jnp.dot(c_act, w_ref[...].astype(BF16), preferred_element_type=F32)


def _modulation(c, w_mod, b_mod):
    depth, d, n = w_mod.shape
    rows = V7X_SUBLANES
    c_pad = jnp.zeros((rows, d), F32).at[:c.shape[0]].set(c)
    tk, tn = min(1024, d), min(2048, n)
    blocks = [((rows, tk), F32), ((tk, tn), F32), ((1, tn), F32), ((rows, tn), F32)]
    return pl.pallas_call(
        _mod_kernel,
        grid=(depth, n // tn, d // tk),
        in_specs=[pl.BlockSpec((rows, tk), lambda l, j, k: (0, k)),
                  pl.BlockSpec((None, tk, tn), lambda l, j, k: (l, k, j)),
                  pl.BlockSpec((None, 1, tn), lambda l, j, k: (l, 0, j))],
        out_specs=pl.BlockSpec((None, rows, tn), lambda l, j, k: (l, 0, j)),
        out_shape=jax.ShapeDtypeStruct((depth, rows, n), F32),
        name="adaln_mod",
        compiler_params=_params(("parallel", "parallel", "arbitrary"), blocks,
                                extra_bytes=_nbytes((tk, tn), BF16)),
    )(c_pad, w_mod, b_mod.reshape(depth, 1, n))


def _modulate_kernel(x_ref, scale_ref, shift_ref, h_ref):
    h_ref[...] = (x_ref[...] * (1.0 + scale_ref[...]) + shift_ref[...]).astype(h_ref.dtype)


def _modulate(x2, scale, shift, seq):
    m, d = x2.shape
    batch = m // seq
    ts = min(256, seq)
    n_s = seq // ts
    vec = pl.BlockSpec((None, 1, d), lambda b, i: (b, 0, 0))
    blocks = [((ts, d), F32), ((ts, d), BF16)]
    return pl.pallas_call(
        _modulate_kernel,
        grid=(batch, n_s),
        in_specs=[pl.BlockSpec((ts, d), lambda b, i: (b * n_s + i, 0)), vec, vec],
        out_specs=pl.BlockSpec((ts, d), lambda b, i: (b * n_s + i, 0)),
        out_shape=jax.ShapeDtypeStruct((m, d), BF16),
        name="modulate",
        compiler_params=_params(("parallel", "parallel"), blocks),
    )(x2, scale, shift)


def _matmul_kernel(a_ref, w_ref, o_ref):
    o_ref[...] = jnp.dot(a_ref[...], w_ref[...], preferred_element_type=F32).astype(o_ref.dtype)


def _matmul(a, w, *, tm, tn, out_dtype, name, n_cols=None):
    m, k = a.shape
    n = w.shape[1] if n_cols is None else n_cols
    tm, tn = min(tm, m), min(tn, n)
    blocks = [((tm, k), a.dtype), ((k, tn), w.dtype), ((tm, tn), out_dtype)]
    return pl.pallas_call(
        _matmul_kernel,
        grid=(m // tm, n // tn),
        in_specs=[pl.BlockSpec((tm, k), lambda i, j: (i, 0)),
                  pl.BlockSpec((k, tn), lambda i, j: (0, j))],
        out_specs=pl.BlockSpec((tm, tn), lambda i, j: (i, j)),
        out_shape=jax.ShapeDtypeStruct((m, n), out_dtype),
        name=name,
        compiler_params=_params(("parallel", "arbitrary"), blocks, extra_bytes=_nbytes((tm, tn), F32)),
    )(a, w)


def _matmul_cat_kernel(a0_ref, a1_ref, w0_ref, w1_ref, o_ref):
    acc = jnp.dot(a0_ref[...], w0_ref[...], preferred_element_type=F32)
    acc += jnp.dot(a1_ref[...], w1_ref[...], preferred_element_type=F32)
    o_ref[...] = acc.astype(o_ref.dtype)


def _matmul_cat(a0, a1, w, *, tm, tn, out_dtype):
    m, k0 = a0.shape
    assert a1.shape == (m, k0) and w.shape[0] == 2 * k0
    n = w.shape[1]
    tm, tn = min(tm, m), min(tn, n)
    blocks = [((tm, 2 * k0), a0.dtype), ((2 * k0, tn), w.dtype), ((tm, tn), out_dtype)]
    return pl.pallas_call(
        _matmul_cat_kernel,
        grid=(m // tm, n // tn),
        in_specs=[pl.BlockSpec((tm, k0), lambda i, j: (i, 0)),
                  pl.BlockSpec((tm, k0), lambda i, j: (i, 0)),
                  pl.BlockSpec((k0, tn), lambda i, j: (0, j)),
                  pl.BlockSpec((k0, tn), lambda i, j: (1, j))],
        out_specs=pl.BlockSpec((tm, tn), lambda i, j: (i, j)),
        out_shape=jax.ShapeDtypeStruct((m, n), out_dtype),
        name="out_proj",
        compiler_params=_params(("parallel", "arbitrary"), blocks, extra_bytes=_nbytes((tm, tn), F32)),
    )(a0, a1, w, w)


def _attn_kernel(slopes_ref, q_ref, k_ref, v_ref, diag_ref, lam_ref, nw_ref, o_ref, qs_ref, m_ref, l_ref,
                 acc_ref, *, tile, rb, rb_diag, lam_init):
    h = pl.program_id(1)
    i = pl.program_id(2)
    slope = slopes_ref[h] * LOG2_E
    dh = ATT_HEAD_DIM
    nt = (((1,), (1,)), ((), ()))
    for mp in range(2):
        q_mp = q_ref[:, mp * dh:(mp + 1) * dh].astype(F32) * (dh ** -0.5 * LOG2_E)
        qs_ref[mp] = q_mp.astype(BF16)

    lanes = V7X_LANES

    def lane_tiled(x, width):
        return jnp.concatenate([x] * (width // lanes), axis=1)

    def update(rows, mp, s, tile_bias, v_t, first):
        nrows, kw = s.shape
        smax = jnp.broadcast_to(jnp.max(s, axis=-1, keepdims=True), (nrows, lanes)) + tile_bias
        if first:
            m_new = smax
        else:
            m_old = m_ref[mp, rows]
            m_new = jnp.maximum(m_old, smax)
            alpha = jnp.exp2(m_old - m_new)
        p = jnp.exp2(s - lane_tiled(m_new - tile_bias, kw))
        psum = p[:, 0:lanes]
        for c in range(1, kw // lanes):
            psum = psum + p[:, c * lanes:(c + 1) * lanes]
        pv = jnp.dot(p.astype(BF16), v_t, preferred_element_type=F32)
        if first:
            l_ref[mp, rows] = psum
            acc_ref[mp, rows] = pv
        else:
            l_ref[mp, rows] = alpha * l_ref[mp, rows] + psum
            acc_ref[mp, rows] = lane_tiled(alpha, pv.shape[1]) * acc_ref[mp, rows] + pv
        m_ref[mp, rows] = m_new

    k_diag = pl.multiple_of(i * tile, tile)
    diag_bias = slope * (i * tile).astype(F32)
    for r in range(tile // rb_diag):
        rows = slice(r * rb_diag, (r + 1) * rb_diag)
        kw = (r + 1) * rb_diag
        bias = slope * diag_ref[rows, 0:kw]
        v_t = v_ref[pl.ds(k_diag, kw), :]
        for mp in range(2):
            kt = k_ref[pl.ds(k_diag, kw), mp * dh:(mp + 1) * dh]
            s = lax.dot_general(qs_ref[mp, rows], kt, nt, preferred_element_type=F32) + bias
            update(rows, mp, s, diag_bias, v_t, first=True)

    col_bias = slope * lax.broadcasted_iota(jnp.int32, (1, tile), 1).astype(F32)

    def body(j, carry):
        k0 = pl.multiple_of(j * tile, tile)
        tile_bias = slope * (j * tile).astype(F32)
        v_t = v_ref[pl.ds(k0, tile), :]
        for r in range(tile // rb):
            rows = slice(r * rb, (r + 1) * rb)
            for mp in range(2):
                kt = k_ref[pl.ds(k0, tile), mp * dh:(mp + 1) * dh]
                s = lax.dot_general(qs_ref[mp, rows], kt, nt, preferred_element_type=F32) + col_bias
                update(rows, mp, s, tile_bias, v_t, first=False)
        return carry

    lax.fori_loop(0, i, body, 0)

    lp = lam_ref[...]
    lam = (jnp.exp(jnp.sum(lp[0:1] * lp[1:2], axis=-1, keepdims=True))
           - jnp.exp(jnp.sum(lp[2:3] * lp[3:4], axis=-1, keepdims=True)) + lam_init)
    inv_l = [1.0 / jnp.sum(l_ref[mp], axis=-1, keepdims=True) for mp in range(2)]
    o = acc_ref[0] * inv_l[0] - lam * (acc_ref[1] * inv_l[1])
    ms = jnp.mean(o * o, axis=-1, keepdims=True)
    y = o * lax.rsqrt(ms + RMS_EPS) * nw_ref[...] * (1.0 - lam_init)
    o_ref[...] = y.astype(o_ref.dtype)


def _attention(proj, lam_params, norm_w, *, seq, n_heads, lam_init):
    m = proj.shape[0]
    batch = m // seq
    tile = min(512, seq)
    n_q = seq // tile
    dv = ATT_VALUE_DIM
    slopes = jnp.asarray(np.array([2.0 ** (-8.0 * (hh + 1) / n_heads) for hh in range(n_heads)],
                                  dtype=np.float32))
    tq = np.arange(tile)[:, None]
    tk = np.arange(tile)[None, :]
    diag_table = jnp.asarray(np.where(tk // CHUNK <= tq // CHUNK, tq - np.abs(tq - tk), DIAG_MASK_PER_SLOPE)
                             .astype(np.float32))
    blocks = [((tile, dv), BF16), ((seq, dv), BF16), ((seq, dv), BF16), ((tile, tile), F32), ((tile, dv), BF16)]
    scratch_bytes = (2 * _nbytes((2, tile, V7X_LANES), F32) + _nbytes((2, tile, dv), F32)
                     + 6 * _nbytes((tile, tile), F32))
    grid_spec = pltpu.PrefetchScalarGridSpec(
        num_scalar_prefetch=1,
        grid=(batch, n_heads, n_q),
        in_specs=[pl.BlockSpec((tile, dv), lambda b, h, i, s: (b * n_q + i, h)),
                  pl.BlockSpec((seq, dv), lambda b, h, i, s: (b, n_heads + h)),
                  pl.BlockSpec((seq, dv), lambda b, h, i, s: (b, 2 * n_heads + h)),
                  pl.BlockSpec((tile, tile), lambda b, h, i, s: (0, 0)),
                  pl.BlockSpec((4, ATT_HEAD_DIM), lambda b, h, i, s: (0, 0)),
                  pl.BlockSpec((1, dv), lambda b, h, i, s: (0, 0))],
        out_specs=pl.BlockSpec((tile, dv), lambda b, h, i, s: (b * n_q + i, h)),
        scratch_shapes=[pltpu.VMEM((2, tile, ATT_HEAD_DIM), BF16), pltpu.VMEM((2, tile, V7X_LANES), F32),
                        pltpu.VMEM((2, tile, V7X_LANES), F32), pltpu.VMEM((2, tile, dv), F32)],
    )
    return pl.pallas_call(
        functools.partial(_attn_kernel, tile=tile, rb=min(ATT_ROW_BLOCK, tile),
                          rb_diag=min(ATT_DIAG_ROW_BLOCK, tile), lam_init=lam_init),
        grid_spec=grid_spec,
        out_shape=jax.ShapeDtypeStruct((m, n_heads * dv), BF16),
        name="diff_attention",
        compiler_params=_params(("parallel", "parallel", "arbitrary"), blocks, extra_bytes=scratch_bytes),
    )(slopes, proj, proj, proj, diag_table, lam_params, norm_w.reshape(1, dv))


def _ssd_kernel(z_ref, xbc_ref, halo_ref, dt_ref, dtt_ref, cw_ref, cb_ref, dtb_l_ref, alog_l_ref,
                dtb_s_ref, alog_s_ref, d_ref, nw_ref, o_ref, state_ref, xe_ref, *, tile, width):
    t = pl.program_id(1)
    gw = width // SSD_GROUPS
    hpg = gw // SSD_HEAD_DIM
    n = SSD_STATE
    b_off, c_off = width, width + SSD_GROUPS * n

    @pl.when(t == 0)
    def _():
        state_ref[...] = jnp.zeros_like(state_ref)

    row = lax.broadcasted_iota(jnp.int32, (tile, tile), 0)
    col = lax.broadcasted_iota(jnp.int32, (tile, tile), 1)
    causal = row >= col
    dt_s = _softplus(dt_ref[...] + dtb_l_ref[...])
    da_s = dt_s * (-jnp.exp(alog_l_ref[...]))
    acs_s = jnp.dot(causal.astype(F32), da_s, precision=lax.Precision.HIGHEST,
                    preferred_element_type=F32)
    dt_l = _softplus(dtt_ref[...] + dtb_s_ref[...])
    da_l = dt_l * (-jnp.exp(alog_s_ref[...]))
    acs_l = jnp.dot(da_l, (row <= col).astype(F32), precision=lax.Precision.HIGHEST,
                    preferred_element_type=F32)

    head_of_lane = lax.broadcasted_iota(jnp.int32, (1, gw), 1) // SSD_HEAD_DIM

    def per_head(vals):
        out = vals[hpg - 1]
        for r in range(hpg - 2, -1, -1):
            out = jnp.where(head_of_lane == r, vals[r], out)
        return out

    halo_rows = halo_ref.shape[0]
    xe_ref[0:halo_rows, :] = jnp.where(t > 0, halo_ref[...].astype(F32), 0.0)
    xe_ref[halo_rows:, :] = xbc_ref[...].astype(F32)

    def conv_silu(c0, w):
        y = cb_ref[:, c0:c0 + w]
        for j in range(SSD_CONV):
            off = halo_rows - (SSD_CONV - 1) + j
            y = y + cw_ref[j:j + 1, c0:c0 + w] * xe_ref[off:off + tile, c0:c0 + w]
        return _silu(y)

    for g in range(SSD_GROUPS):
        xs = conv_silu(g * gw, gw)
        bm = conv_silu(b_off + g * n, n).astype(BF16)
        cm = conv_silu(c_off + g * n, n).astype(BF16)
        heads = [g * hpg + r for r in range(hpg)]
        acs_cols = [acs_s[:, hh:hh + 1] for hh in heads]
        xdt = xs * per_head([dt_s[:, hh:hh + 1] for hh in heads])
        cb = lax.dot_general(cm, bm, (((1,), (1,)), ((), ())), preferred_element_type=F32)
        prev = state_ref[g]
        y = jnp.dot(cm, prev.astype(BF16), preferred_element_type=F32)
        y = y * per_head([jnp.exp(a) for a in acs_cols])
        for r, hh in enumerate(heads):
            seg = acs_cols[r] - acs_l[hh:hh + 1, :]
            decay = jnp.exp(jnp.where(causal, seg, MASK_VALUE))
            x_r = jnp.where(head_of_lane == r, xdt, 0.0).astype(BF16)
            y = y + jnp.dot((cb * decay).astype(BF16), x_r, preferred_element_type=F32)
        last = [a[tile - 1:tile, :] for a in acs_cols]
        to_end = per_head([jnp.exp(last[r] - acs_cols[r]) for r in range(hpg)])
        new_state = lax.dot_general(bm, (xdt * to_end).astype(BF16), (((0,), (0,)), ((), ())),
                                    preferred_element_type=F32)
        state_ref[g] = prev * per_head([jnp.exp(a) for a in last]) + new_state
        lanes = slice(g * gw, (g + 1) * gw)
        y = y + d_ref[:, lanes] * xs
        y = y * _silu(z_ref[:, lanes].astype(F32))
        y = y * lax.rsqrt(jnp.mean(y * y, axis=-1, keepdims=True) + RMS_EPS) * nw_ref[:, lanes]
        o_ref[:, lanes] = y.astype(o_ref.dtype)


def _ssd(proj, dt_raw, conv_w, conv_b, dt_bias, a_log, d_skip, norm_w, *, seq, z_col, xbc_col):
    m = proj.shape[0]
    batch = m // seq
    n_heads = dt_bias.shape[0]
    width = n_heads * SSD_HEAD_DIM
    conv_dim = conv_w.shape[1]
    tile = min(256, seq)
    n_t = seq // tile
    halo = V7X_BF16_SUBLANES
    halo_per_tile = tile // halo
    lanes = dt_raw.shape[1]

    def lane_vec(v):
        return jnp.zeros((1, lanes), F32).at[0, :n_heads].set(v)

    dtt = dt_raw[:, :n_heads].T
    const = lambda b, t: (0, 0)
    blocks = [((tile, width), BF16), ((tile, conv_dim), BF16), ((halo, conv_dim), BF16),
              ((tile, lanes), F32), ((n_heads, tile), F32), ((tile, width), BF16)]
    scratch_bytes = _nbytes((SSD_GROUPS, SSD_STATE, width // SSD_GROUPS), F32) + 12 * _nbytes((tile, tile), F32) \
        + 2 * _nbytes((tile + halo, conv_dim), F32)
    return pl.pallas_call(
        functools.partial(_ssd_kernel, tile=tile, width=width),
        grid=(batch, n_t),
        in_specs=[pl.BlockSpec((tile, width), lambda b, t: (b * n_t + t, z_col)),
                  pl.BlockSpec((tile, conv_dim), lambda b, t: (b * n_t + t, xbc_col)),
                  pl.BlockSpec((halo, conv_dim),
                               lambda b, t: (jnp.maximum((b * n_t + t) * halo_per_tile - 1, 0), xbc_col)),
                  pl.BlockSpec((tile, lanes), lambda b, t: (b * n_t + t, 0)),
                  pl.BlockSpec((n_heads, tile), lambda b, t: (0, b * n_t + t)),
                  pl.BlockSpec((SSD_CONV, conv_dim), const),
                  pl.BlockSpec((1, conv_dim), const),
                  pl.BlockSpec((1, lanes), const),
                  pl.BlockSpec((1, lanes), const),
                  pl.BlockSpec((n_heads, 1), const),
                  pl.BlockSpec((n_heads, 1), const),
                  pl.BlockSpec((1, width), const),
                  pl.BlockSpec((1, width), const)],
        out_specs=pl.BlockSpec((tile, width), lambda b, t: (b * n_t + t, 0)),
        out_shape=jax.ShapeDtypeStruct((m, width), BF16),
        name="ssd_mixer",
        scratch_shapes=[pltpu.VMEM((SSD_GROUPS, SSD_STATE, width // SSD_GROUPS), F32),
                        pltpu.VMEM((tile + halo, conv_dim), F32)],
        compiler_params=_params(("parallel", "arbitrary"), blocks, extra_bytes=scratch_bytes),
    )(proj, proj, proj, dt_raw, dtt, conv_w, conv_b.reshape(1, conv_dim), lane_vec(dt_bias), lane_vec(a_log),
      dt_bias.reshape(n_heads, 1), a_log.reshape(n_heads, 1),
      jnp.repeat(d_skip, SSD_HEAD_DIM).reshape(1, width), norm_w.reshape(1, width))


def _ln_kernel(*refs, alpha, emit_next):
    if emit_next:
        x_ref, br_ref, gate_ref, g_ref, b_ref, scale_ref, shift_ref, xo_ref, h_ref = refs
    else:
        x_ref, br_ref, gate_ref, g_ref, b_ref, xo_ref = refs
    v = alpha * x_ref[...] + gate_ref[...] * br_ref[...].astype(F32)
    mu = jnp.mean(v, axis=-1, keepdims=True)
    dev = v - mu
    var = jnp.mean(dev * dev, axis=-1, keepdims=True)
    y = dev * lax.rsqrt(var + LN_EPS) * g_ref[...] + b_ref[...]
    xo_ref[...] = y
    if emit_next:
        h_ref[...] = (y * (1.0 + scale_ref[...]) + shift_ref[...]).astype(h_ref.dtype)


def _residual_ln(x2, branch, gate, ln_g, ln_b, next_scale, next_shift, *, seq, alpha):
    m, d = x2.shape
    batch = m // seq
    ts = min(256, seq)
    n_s = seq // ts
    emit_next = next_scale is not None
    rows = pl.BlockSpec((ts, d), lambda b, i: (b * n_s + i, 0))
    per_batch = pl.BlockSpec((None, 1, d), lambda b, i: (b, 0, 0))
    shared = pl.BlockSpec((1, d), lambda b, i: (0, 0))
    in_specs = [rows, rows, per_batch, shared, shared]
    args = [x2, branch, gate, ln_g.reshape(1, d), ln_b.reshape(1, d)]
    out_specs = [rows]
    out_shape = [jax.ShapeDtypeStruct((m, d), F32)]
    blocks = [((ts, d), F32), ((ts, d), branch.dtype), ((ts, d), F32)]
    if emit_next:
        in_specs += [per_batch, per_batch]
        args += [next_scale, next_shift]
        out_specs.append(rows)
        out_shape.append(jax.ShapeDtypeStruct((m, d), BF16))
        blocks.append(((ts, d), BF16))
    out = pl.pallas_call(
        functools.partial(_ln_kernel, alpha=alpha, emit_next=emit_next),
        grid=(batch, n_s),
        in_specs=in_specs,
        out_specs=out_specs,
        out_shape=out_shape,
        name="residual_ln",
        compiler_params=_params(("parallel", "parallel"), blocks, extra_bytes=2 * _nbytes((ts, d), F32)),
    )(*args)
    return (out[0], out[1]) if emit_next else (out[0], None)


def _ffn_up_kernel(h_ref, wg_ref, wu_ref, cw_ref, cb_ref, o_ref, carry_ref, *, tm, rc, tiles_per_seq):
    i = pl.program_id(0)
    j = pl.program_id(1)
    tn = o_ref.shape[1]
    tail = jnp.where(i % tiles_per_seq == 0, 0.0, carry_ref[j])
    ridx = lax.broadcasted_iota(jnp.int32, (rc, tn), 0)
    for r in range(tm // rc):
        rows = slice(r * rc, (r + 1) * rc)
        gate = jnp.dot(h_ref[rows, :], wg_ref[...], preferred_element_type=F32)
        up = jnp.dot(h_ref[rows, :], wu_ref[...], preferred_element_type=F32)
        p1 = tail[V7X_SUBLANES - 1:V7X_SUBLANES]
        p2 = tail[V7X_SUBLANES - 2:V7X_SUBLANES - 1]
        g1 = jnp.where(ridx == 0, p1, pltpu.roll(gate, 1, 0))
        g2 = jnp.where(ridx == 0, p2, jnp.where(ridx == 1, p1, pltpu.roll(gate, 2, 0)))
        y = cb_ref[...] + cw_ref[0:1, :] * g2 + cw_ref[1:2, :] * g1 + cw_ref[2:3, :] * gate
        o_ref[rows, :] = (_silu(y) * up).astype(o_ref.dtype)
        tail = gate[rc - V7X_SUBLANES:, :]
    carry_ref[j] = tail


def _ffn_up(h, w_gate, w_up, conv_w, conv_b, *, seq, tn):
    m, d = h.shape
    f = w_gate.shape[1]
    tm = min(1024, seq)
    rc = min(512, tm)
    n_j = f // tn
    blocks = [((tm, d), BF16), ((d, tn), BF16), ((d, tn), BF16), ((tm, tn), BF16)]
    w_spec = pl.BlockSpec((d, tn), lambda i, j: (0, j))
    return pl.pallas_call(
        functools.partial(_ffn_up_kernel, tm=tm, rc=rc, tiles_per_seq=seq // tm),
        grid=(m // tm, n_j),
        in_specs=[pl.BlockSpec((tm, d), lambda i, j: (i, 0)), w_spec, w_spec,
                  pl.BlockSpec((FFN_CONV, tn), lambda i, j: (0, j)),
                  pl.BlockSpec((1, tn), lambda i, j: (0, j))],
        out_specs=pl.BlockSpec((tm, tn), lambda i, j: (i, j)),
        out_shape=jax.ShapeDtypeStruct((m, f), BF16),
        name="ffn_up",
        scratch_shapes=[pltpu.VMEM((n_j, V7X_SUBLANES, tn), F32)],
        compiler_params=_params(("arbitrary", "arbitrary"), blocks,
                                extra_bytes=8 * _nbytes((rc, tn), F32)),
    )(h, w_gate, w_up, conv_w, conv_b.reshape(1, f))


def kernel(x, c, w_mod, b_mod, w_in, diff_lambda, diff_norm_w, ssd_conv_w, ssd_conv_b, ssd_dt_bias,
           ssd_a_log, ssd_d, ssd_norm_w, w_out, ln1_g, ln1_b, w_gate, w_up, ffn_conv_w, ffn_conv_b,
           w_down, ln2_g, ln2_b):
    batch, seq, d = x.shape
    depth = w_mod.shape[0]
    m = batch * seq
    n_ssd_heads = ssd_dt_bias.shape[1]
    ssd_width = n_ssd_heads * SSD_HEAD_DIM
    conv_dim = ssd_conv_w.shape[2]
    n_in = w_in.shape[2]
    att_width = (n_in - n_ssd_heads - conv_dim - ssd_width) // 3
    n_att_heads = att_width // ATT_VALUE_DIM
    n_main = n_in - n_ssd_heads
    assert att_width == ssd_width and conv_dim == 2 * ssd_width
    alpha = (2.0 * depth) ** 0.25
    ffn_tn = 256

    mod = _modulation(c, w_mod, b_mod)[:, :batch]
    mod = mod.reshape(depth, batch, 6, 1, d)
    shift1, scale1, gate1, shift2, scale2, gate2 = (mod[:, :, s] for s in range(6))

    x2 = x.reshape(m, d)
    h = _modulate(x2, scale1[0], shift1[0], seq)
    for l in range(depth):
        w_in_bf = w_in[l].astype(BF16)
        w_dt = jnp.pad(w_in_bf[:, n_main:], ((0, 0), (0, V7X_LANES - n_ssd_heads)))
        proj = _matmul(h, w_in_bf, tm=1024, tn=1024, out_dtype=BF16, name="in_proj", n_cols=n_main)
        dt_raw = _matmul(h, w_dt, tm=1024, tn=V7X_LANES, out_dtype=F32, name="dt_proj")
        lam_init = 0.8 - 0.6 * float(np.exp(-0.3 * l))
        y_att = _attention(proj, diff_lambda[l], diff_norm_w[l], seq=seq, n_heads=n_att_heads,
                           lam_init=lam_init)
        y_ssd = _ssd(proj, dt_raw, ssd_conv_w[l], ssd_conv_b[l], ssd_dt_bias[l], ssd_a_log[l], ssd_d[l],
                     ssd_norm_w[l], seq=seq, z_col=3 * att_width // ssd_width,
                     xbc_col=(3 * att_width + ssd_width) // conv_dim)
        mix = _matmul_cat(y_att, y_ssd, w_out[l].astype(BF16), tm=1024, tn=1024, out_dtype=BF16)
        x2, h = _residual_ln(x2, mix, gate1[l], ln1_g[l], ln1_b[l], scale2[l], shift2[l],
                             seq=seq, alpha=alpha)
        act = _ffn_up(h, w_gate[l].astype(BF16), w_up[l].astype(BF16), ffn_conv_w[l], ffn_conv_b[l],
                      seq=seq, tn=ffn_tn)
        ffn = _matmul(act, w_down[l].astype(BF16), tm=512, tn=512, out_dtype=BF16, name="ffn_down")
        last = l == depth - 1
        x2, h = _residual_ln(x2, ffn, gate2[l], ln2_g[l], ln2_b[l],
                             None if last else scale1[l + 1], None if last else shift1[l + 1],
                             seq=seq, alpha=alpha)
    return x2.reshape(batch, seq, d)
```

```python
import functools
from typing import Callable, NamedTuple, Tuple

import numpy as np
import jax
import jax.numpy as jnp
from jax import lax
from jax.experimental import pallas as pl
from jax.experimental.pallas import tpu as pltpu

F32 = jnp.float32
BF16 = jnp.bfloat16

CHUNK = 64
ATT_HEAD_DIM = 128
ATT_VALUE_DIM = 2 * ATT_HEAD_DIM
SSD_HEAD_DIM = 64
SSD_GROUPS = 8
SSD_STATE = 128
SSD_CONV = 4
FFN_CONV = 3
LN_EPS = 1e-5
RMS_EPS = 1e-5

V7X_LANES = 128
V7X_SUBLANES = 8
V7X_BF16_SUBLANES = 16
V7X_VMEM_BYTES = 64 * 1024 * 1024
VMEM_REQUEST_CAP = V7X_VMEM_BYTES - 8 * 1024 * 1024

MASK_VALUE = -1e30
LOG2_E = 1.4426950408889634
ATT_ROW_BLOCK = 128
ATT_DIAG_ROW_BLOCK = 256
DIAG_MASK_PER_SLOPE = -1e32


def _nbytes(shape, dtype):
    return int(np.prod(shape)) * jnp.dtype(dtype).itemsize


def _params(semantics, blocks, extra_bytes=0):
    need = 2 * sum(_nbytes(s, d) for s, d in blocks) + extra_bytes + (4 << 20)
    return pltpu.CompilerParams(dimension_semantics=semantics,
                                vmem_limit_bytes=int(min(max(need, 16 << 20), VMEM_REQUEST_CAP)))


def _silu(v):
    return v * (0.5 * jnp.tanh(0.5 * v) + 0.5)


def _softplus(v):
    return jnp.maximum(v, 0.0) + jnp.log1p(jnp.exp(-jnp.abs(v)))


class _CastRider(NamedTuple):
    weight: jax.Array
    layer: int
    block: Tuple[int, int]
    index: Callable


def _rider_specs(riders, n_trailing_index_args=0):
    in_specs, out_specs, out_shapes = [], [], []
    for rd in riders:
        def idx(*g, rd=rd, lead=None):
            g = g[:len(g) - n_trailing_index_args] if n_trailing_index_args else g
            return (lead,) + tuple(rd.index(*g))
        in_specs.append(pl.BlockSpec((None,) + rd.block, functools.partial(idx, lead=rd.layer)))
        out_specs.append(pl.BlockSpec((None,) + rd.block, functools.partial(idx, lead=0)))
        out_shapes.append(jax.ShapeDtypeStruct((1,) + rd.weight.shape[1:], BF16))
    return in_specs, out_specs, out_shapes


def _rider_blocks(riders):
    return [(rd.block, F32) for rd in riders] + [(rd.block, BF16) for rd in riders]


def _with_riders(body, n_leading, n_out, n_riders):
    if not n_riders:
        return body

    def wrapped(*refs):
        rider_in = refs[n_leading:n_leading + n_riders]
        host_out = refs[n_leading + n_riders:n_leading + n_riders + n_out]
        rider_out = refs[n_leading + n_riders + n_out:n_leading + 2 * n_riders + n_out]
        for src, dst in zip(rider_in, rider_out):
            dst[...] = src[...].astype(dst.dtype)
        body(*refs[:n_leading], *host_out, *refs[n_leading + 2 * n_riders + n_out:])

    return wrapped


def _mod_kernel(c_ref, w_ref, b_ref, o_ref):
    @pl.when(pl.program_id(2) == 0)
    def _():
        o_ref[...] = jnp.broadcast_to(b_ref[...], o_ref.shape)

    c_act = _silu(c_ref[...]).astype(BF16)
    o_ref[...] += jnp.dot(c_act, w_ref[...].astype(BF16), preferred_element_type=F32)


def _modulation(c, w_mod, b_mod):
    depth, d, n = w_mod.shape
    rows = V7X_SUBLANES
    c_pad = jnp.zeros((rows, d), F32).at[:c.shape[0]].set(c)
    tk, tn = min(1024, d), min(2048, n)
    blocks = [((rows, tk), F32), ((tk, tn), F32), ((1, tn), F32), ((rows, tn), F32)]
    return pl.pallas_call(
        _mod_kernel,
        grid=(depth, n // tn, d // tk),
        in_specs=[pl.BlockSpec((rows, tk), lambda l, j, k: (0, k)),
                  pl.BlockSpec((None, tk, tn), lambda l, j, k: (l, k, j)),
                  pl.BlockSpec((None, 1, tn), lambda l, j, k: (l, 0, j))],
        out_specs=pl.BlockSpec((None, rows, tn), lambda l, j, k: (l, 0, j)),
        out_shape=jax.ShapeDtypeStruct((depth, rows, n), F32),
        name="adaln_mod",
        compiler_params=_params(("parallel", "parallel", "arbitrary"), blocks,
                                extra_bytes=_nbytes((tk, tn), BF16)),
    )(c_pad, w_mod, b_mod.reshape(depth, 1, n))


def _modulate_kernel(x_ref, scale_ref, shift_ref, h_ref):
    h_ref[...] = (x_ref[...] * (1.0 + scale_ref[...]) + shift_ref[...]).astype(h_ref.dtype)


def _modulate(x2, scale, shift, seq):
    m, d = x2.shape
    batch = m // seq
    ts = min(256, seq)
    n_s = seq // ts
    vec = pl.BlockSpec((None, 1, d), lambda b, i: (b, 0, 0))
    blocks = [((ts, d), F32), ((ts, d), BF16)]
    return pl.pallas_call(
        _modulate_kernel,
        grid=(batch, n_s),
        in_specs=[pl.BlockSpec((ts, d), lambda b, i: (b * n_s + i, 0)), vec, vec],
        out_specs=pl.BlockSpec((ts, d), lambda b, i: (b * n_s + i, 0)),
        out_shape=jax.ShapeDtypeStruct((m, d), BF16),
        name="modulate",
        compiler_params=_params(("parallel", "parallel"), blocks),
    )(x2, scale, shift)


def _matmul_kernel(a_ref, w_ref, o_ref):
    o_ref[...] = jnp.dot(a_ref[...], w_ref[...], preferred_element_type=F32).astype(o_ref.dtype)


def _matmul(a, w, layer, *, tm, tn, out_dtype, name, n_cols=None, make_riders=None):
    m, k = a.shape
    n = w.shape[2] if n_cols is None else n_cols
    tm, tn = min(tm, m), min(tn, n)
    riders = make_riders((m // tm, n // tn)) if make_riders else ()
    r_in, r_out, r_shapes = _rider_specs(riders)
    blocks = [((tm, k), a.dtype), ((k, tn), w.dtype), ((tm, tn), out_dtype)] + _rider_blocks(riders)
    out = pl.pallas_call(
        _with_riders(_matmul_kernel, 2, 1, len(riders)),
        grid=(m // tm, n // tn),
        in_specs=[pl.BlockSpec((tm, k), lambda i, j: (i, 0)),
                  pl.BlockSpec((None, k, tn), lambda i, j: (layer, 0, j))] + r_in,
        out_specs=[pl.BlockSpec((tm, tn), lambda i, j: (i, j))] + r_out,
        out_shape=[jax.ShapeDtypeStruct((m, n), out_dtype)] + r_shapes,
        name=name,
        compiler_params=_params(("arbitrary", "arbitrary"), blocks, extra_bytes=_nbytes((tm, tn), F32)),
    )(a, w, *[rd.weight for rd in riders])
    return (out[0], out[1:]) if riders else out[0]


def _matmul_cat_kernel(a0_ref, a1_ref, w0_ref, w1_ref, o_ref):
    acc = jnp.dot(a0_ref[...], w0_ref[...], preferred_element_type=F32)
    acc += jnp.dot(a1_ref[...], w1_ref[...], preferred_element_type=F32)
    o_ref[...] = acc.astype(o_ref.dtype)


def _matmul_cat(a0, a1, w, layer, *, tm, tn, out_dtype):
    m, k0 = a0.shape
    assert a1.shape == (m, k0) and w.shape[1] == 2 * k0
    n = w.shape[2]
    tm, tn = min(tm, m), min(tn, n)
    blocks = [((tm, 2 * k0), a0.dtype), ((2 * k0, tn), w.dtype), ((tm, tn), out_dtype)]
    return pl.pallas_call(
        _matmul_cat_kernel,
        grid=(m // tm, n // tn),
        in_specs=[pl.BlockSpec((tm, k0), lambda i, j: (i, 0)),
                  pl.BlockSpec((tm, k0), lambda i, j: (i, 0)),
                  pl.BlockSpec((None, k0, tn), lambda i, j: (layer, 0, j)),
                  pl.BlockSpec((None, k0, tn), lambda i, j: (layer, 1, j))],
        out_specs=pl.BlockSpec((tm, tn), lambda i, j: (i, j)),
        out_shape=jax.ShapeDtypeStruct((m, n), out_dtype),
        name="out_proj",
        compiler_params=_params(("parallel", "arbitrary"), blocks, extra_bytes=_nbytes((tm, tn), F32)),
    )(a0, a1, w, w)


def _attn_kernel(slopes_ref, q_ref, k_ref, v_ref, diag_ref, lam_ref, nw_ref, o_ref, qs_ref, m_ref, l_ref,
                 acc_ref, *, tile, rb, rb_diag, lam_init):
    h = pl.program_id(1)
    i = pl.program_id(2)
    slope = slopes_ref[h] * LOG2_E
    dh = ATT_HEAD_DIM
    nt = (((1,), (1,)), ((), ()))
    for mp in range(2):
        q_mp = q_ref[:, mp * dh:(mp + 1) * dh].astype(F32) * (dh ** -0.5 * LOG2_E)
        qs_ref[mp] = q_mp.astype(BF16)

    lanes = V7X_LANES

    def lane_tiled(x, width):
        return jnp.concatenate([x] * (width // lanes), axis=1)

    def update(rows, mp, s, tile_bias, v_t, first):
        nrows, kw = s.shape
        smax = jnp.broadcast_to(jnp.max(s, axis=-1, keepdims=True), (nrows, lanes)) + tile_bias
        if first:
            m_new = smax
        else:
            m_old = m_ref[mp, rows]
            m_new = jnp.maximum(m_old, smax)
            alpha = jnp.exp2(m_old - m_new)
        p = jnp.exp2(s - lane_tiled(m_new - tile_bias, kw))
        psum = p[:, 0:lanes]
        for c in range(1, kw // lanes):
            psum = psum + p[:, c * lanes:(c + 1) * lanes]
        pv = jnp.dot(p.astype(BF16), v_t, preferred_element_type=F32)
        if first:
            l_ref[mp, rows] = psum
            acc_ref[mp, rows] = pv
        else:
            l_ref[mp, rows] = alpha * l_ref[mp, rows] + psum
            acc_ref[mp, rows] = lane_tiled(alpha, pv.shape[1]) * acc_ref[mp, rows] + pv
        m_ref[mp, rows] = m_new

    k_diag = pl.multiple_of(i * tile, tile)
    diag_bias = slope * (i * tile).astype(F32)
    for r in range(tile // rb_diag):
        rows = slice(r * rb_diag, (r + 1) * rb_diag)
        kw = (r + 1) * rb_diag
        bias = slope * diag_ref[rows, 0:kw]
        v_t = v_ref[pl.ds(k_diag, kw), :]
        for mp in range(2):
            kt = k_ref[pl.ds(k_diag, kw), mp * dh:(mp + 1) * dh]
            s = lax.dot_general(qs_ref[mp, rows], kt, nt, preferred_element_type=F32) + bias
            update(rows, mp, s, diag_bias, v_t, first=True)

    col_bias = slope * lax.broadcasted_iota(jnp.int32, (1, 2 * tile), 1).astype(F32)

    def visible_keys(k0, kw):
        tile_bias = slope * k0.astype(F32)
        v_t = v_ref[pl.ds(k0, kw), :]
        for r in range(tile // rb):
            rows = slice(r * rb, (r + 1) * rb)
            for mp in range(2):
                kt = k_ref[pl.ds(k0, kw), mp * dh:(mp + 1) * dh]
                s = lax.dot_general(qs_ref[mp, rows], kt, nt, preferred_element_type=F32)
                update(rows, mp, s + col_bias[:, :kw], tile_bias, v_t, first=False)

    def body(jj, carry):
        visible_keys(pl.multiple_of(jj * (2 * tile), 2 * tile), 2 * tile)
        return carry

    lax.fori_loop(0, i // 2, body, 0)

    @pl.when(i % 2 == 1)
    def _():
        visible_keys(pl.multiple_of((i - 1) * tile, tile), tile)

    lp = lam_ref[...]
    lam = (jnp.exp(jnp.sum(lp[0:1] * lp[1:2], axis=-1, keepdims=True))
           - jnp.exp(jnp.sum(lp[2:3] * lp[3:4], axis=-1, keepdims=True)) + lam_init)
    inv_l = [1.0 / jnp.sum(l_ref[mp], axis=-1, keepdims=True) for mp in range(2)]
    o = acc_ref[0] * inv_l[0] - lam * (acc_ref[1] * inv_l[1])
    ms = jnp.mean(o * o, axis=-1, keepdims=True)
    y = o * lax.rsqrt(ms + RMS_EPS) * nw_ref[...] * (1.0 - lam_init)
    o_ref[...] = y.astype(o_ref.dtype)


def _attention(proj, lam_params, norm_w, *, seq, n_heads, lam_init, make_riders=None):
    m = proj.shape[0]
    batch = m // seq
    tile = min(512, seq)
    n_q = seq // tile
    dv = ATT_VALUE_DIM
    slopes = jnp.asarray(np.array([2.0 ** (-8.0 * (hh + 1) / n_heads) for hh in range(n_heads)],
                                  dtype=np.float32))
    tq = np.arange(tile)[:, None]
    tk = np.arange(tile)[None, :]
    diag_table = jnp.asarray(np.where(tk // CHUNK <= tq // CHUNK, tq - np.abs(tq - tk), DIAG_MASK_PER_SLOPE)
                             .astype(np.float32))
    riders = make_riders((batch, n_heads, n_q)) if make_riders else ()
    r_in, r_out, r_shapes = _rider_specs(riders, n_trailing_index_args=1)
    blocks = [((tile, dv), BF16), ((seq, dv), BF16), ((seq, dv), BF16), ((tile, tile), F32),
              ((tile, dv), BF16)] + _rider_blocks(riders)
    scratch_bytes = (2 * _nbytes((2, tile, V7X_LANES), F32) + _nbytes((2, tile, dv), F32)
                     + 6 * _nbytes((tile, tile), F32))
    grid_spec = pltpu.PrefetchScalarGridSpec(
        num_scalar_prefetch=1,
        grid=(batch, n_heads, n_q),
        in_specs=[pl.BlockSpec((tile, dv), lambda b, h, i, s: (b * n_q + i, h)),
                  pl.BlockSpec((seq, dv), lambda b, h, i, s: (b, n_heads + h)),
                  pl.BlockSpec((seq, dv), lambda b, h, i, s: (b, 2 * n_heads + h)),
                  pl.BlockSpec((tile, tile), lambda b, h, i, s: (0, 0)),
                  pl.BlockSpec((4, ATT_HEAD_DIM), lambda b, h, i, s: (0, 0)),
                  pl.BlockSpec((1, dv), lambda b, h, i, s: (0, 0))] + r_in,
        out_specs=[pl.BlockSpec((tile, dv), lambda b, h, i, s: (b * n_q + i, h))] + r_out,
        scratch_shapes=[pltpu.VMEM((2, tile, ATT_HEAD_DIM), BF16), pltpu.VMEM((2, tile, V7X_LANES), F32),
                        pltpu.VMEM((2, tile, V7X_LANES), F32), pltpu.VMEM((2, tile, dv), F32)],
    )
    body = functools.partial(_attn_kernel, tile=tile, rb=min(ATT_ROW_BLOCK, tile),
                             rb_diag=min(ATT_DIAG_ROW_BLOCK, tile), lam_init=lam_init)
    out = pl.pallas_call(
        _with_riders(body, 7, 1, len(riders)),
        grid_spec=grid_spec,
        out_shape=[jax.ShapeDtypeStruct((m, n_heads * dv), BF16)] + r_shapes,
        name="diff_attention",
        compiler_params=_params(("arbitrary", "arbitrary", "arbitrary"), blocks, extra_bytes=scratch_bytes),
    )(slopes, proj, proj, proj, diag_table, lam_params, norm_w.reshape(1, dv), *[rd.weight for rd in riders])
    return (out[0], out[1:]) if riders else out[0]


def _ssd_kernel(z_ref, xbc_ref, halo_ref, dt_ref, dtt_ref, cw_ref, cb_ref, dtb_l_ref, alog_l_ref,
                dtb_s_ref, alog_s_ref, d_ref, nw_ref, o_ref, state_ref, xe_ref, *, tile, width):
    t = pl.program_id(1)
    gw = width // SSD_GROUPS
    hpg = gw // SSD_HEAD_DIM
    n = SSD_STATE
    b_off, c_off = width, width + SSD_GROUPS * n

    @pl.when(t == 0)
    def _():
        state_ref[...] = jnp.zeros_like(state_ref)

    row = lax.broadcasted_iota(jnp.int32, (tile, tile), 0)
    col = lax.broadcasted_iota(jnp.int32, (tile, tile), 1)
    causal = row >= col
    dt_s = _softplus(dt_ref[...] + dtb_l_ref[...])
    da_s = dt_s * (-jnp.exp(alog_l_ref[...]))
    acs_s = jnp.dot(causal.astype(F32), da_s, precision=lax.Precision.HIGHEST,
                    preferred_element_type=F32)
    dt_l = _softplus(dtt_ref[...] + dtb_s_ref[...])
    da_l = dt_l * (-jnp.exp(alog_s_ref[...]))
    acs_l = jnp.dot(da_l, (row <= col).astype(F32), precision=lax.Precision.HIGHEST,
                    preferred_element_type=F32)

    head_of_lane = lax.broadcasted_iota(jnp.int32, (1, gw), 1) // SSD_HEAD_DIM

    def per_head(vals):
        out = vals[hpg - 1]
        for r in range(hpg - 2, -1, -1):
            out = jnp.where(head_of_lane == r, vals[r], out)
        return out

    halo_rows = halo_ref.shape[0]
    xe_ref[0:halo_rows, :] = jnp.where(t > 0, halo_ref[...].astype(F32), 0.0)
    xe_ref[halo_rows:, :] = xbc_ref[...].astype(F32)

    def conv_silu(c0, w):
        y = cb_ref[:, c0:c0 + w]
        for j in range(SSD_CONV):
            off = halo_rows - (SSD_CONV - 1) + j
            y = y + cw_ref[j:j + 1, c0:c0 + w] * xe_ref[off:off + tile, c0:c0 + w]
        return _silu(y)

    for g in range(SSD_GROUPS):
        xs = conv_silu(g * gw, gw)
        bm = conv_silu(b_off + g * n, n).astype(BF16)
        cm = conv_silu(c_off + g * n, n).astype(BF16)
        heads = [g * hpg + r for r in range(hpg)]
        acs_cols = [acs_s[:, hh:hh + 1] for hh in heads]
        xdt = xs * per_head([dt_s[:, hh:hh + 1] for hh in heads])
        cb = lax.dot_general(cm, bm, (((1,), (1,)), ((), ())), preferred_element_type=F32)
        prev = state_ref[g]
        y = jnp.dot(cm, prev.astype(BF16), preferred_element_type=F32)
        y = y * per_head([jnp.exp(a) for a in acs_cols])
        for r, hh in enumerate(heads):
            seg = acs_cols[r] - acs_l[hh:hh + 1, :]
            decay = jnp.exp(jnp.where(causal, seg, MASK_VALUE))
            x_r = jnp.where(head_of_lane == r, xdt, 0.0).astype(BF16)
            y = y + jnp.dot((cb * decay).astype(BF16), x_r, preferred_element_type=F32)
        last = [a[tile - 1:tile, :] for a in acs_cols]
        to_end = per_head([jnp.exp(last[r] - acs_cols[r]) for r in range(hpg)])
        new_state = lax.dot_general(bm, (xdt * to_end).astype(BF16), (((0,), (0,)), ((), ())),
                                    preferred_element_type=F32)
        state_ref[g] = prev * per_head([jnp.exp(a) for a in last]) + new_state
        lanes = slice(g * gw, (g + 1) * gw)
        y = y + d_ref[:, lanes] * xs
        y = y * _silu(z_ref[:, lanes].astype(F32))
        y = y * lax.rsqrt(jnp.mean(y * y, axis=-1, keepdims=True) + RMS_EPS) * nw_ref[:, lanes]
        o_ref[:, lanes] = y.astype(o_ref.dtype)


def _ssd(proj, dt_raw, conv_w, conv_b, dt_bias, a_log, d_skip, norm_w, *, seq, z_col, xbc_col, make_riders=None):
    m = proj.shape[0]
    batch = m // seq
    n_heads = dt_bias.shape[0]
    width = n_heads * SSD_HEAD_DIM
    conv_dim = conv_w.shape[1]
    tile = min(256, seq)
    n_t = seq // tile
    halo = V7X_BF16_SUBLANES
    halo_per_tile = tile // halo
    lanes = dt_raw.shape[1]

    def lane_vec(v):
        return jnp.zeros((1, lanes), F32).at[0, :n_heads].set(v)

    dtt = dt_raw[:, :n_heads].T
    const = lambda b, t: (0, 0)
    riders = make_riders((batch, n_t)) if make_riders else ()
    r_in, r_out, r_shapes = _rider_specs(riders)
    blocks = [((tile, width), BF16), ((tile, conv_dim), BF16), ((halo, conv_dim), BF16),
              ((tile, lanes), F32), ((n_heads, tile), F32), ((tile, width), BF16)] + _rider_blocks(riders)
    scratch_bytes = _nbytes((SSD_GROUPS, SSD_STATE, width // SSD_GROUPS), F32) + 12 * _nbytes((tile, tile), F32) \
        + 2 * _nbytes((tile + halo, conv_dim), F32)
    out = pl.pallas_call(
        _with_riders(functools.partial(_ssd_kernel, tile=tile, width=width), 13, 1, len(riders)),
        grid=(batch, n_t),
        in_specs=[pl.BlockSpec((tile, width), lambda b, t: (b * n_t + t, z_col)),
                  pl.BlockSpec((tile, conv_dim), lambda b, t: (b * n_t + t, xbc_col)),
                  pl.BlockSpec((halo, conv_dim),
                               lambda b, t: (jnp.maximum((b * n_t + t) * halo_per_tile - 1, 0), xbc_col)),
                  pl.BlockSpec((tile, lanes), lambda b, t: (b * n_t + t, 0)),
                  pl.BlockSpec((n_heads, tile), lambda b, t: (0, b * n_t + t)),
                  pl.BlockSpec((SSD_CONV, conv_dim), const),
                  pl.BlockSpec((1, conv_dim), const),
                  pl.BlockSpec((1, lanes), const),
                  pl.BlockSpec((1, lanes), const),
                  pl.BlockSpec((n_heads, 1), const),
                  pl.BlockSpec((n_heads, 1), const),
                  pl.BlockSpec((1, width), const),
                  pl.BlockSpec((1, width), const)] + r_in,
        out_specs=[pl.BlockSpec((tile, width), lambda b, t: (b * n_t + t, 0))] + r_out,
        out_shape=[jax.ShapeDtypeStruct((m, width), BF16)] + r_shapes,
        name="ssd_mixer",
        scratch_shapes=[pltpu.VMEM((SSD_GROUPS, SSD_STATE, width // SSD_GROUPS), F32),
                        pltpu.VMEM((tile + halo, conv_dim), F32)],
        compiler_params=_params(("arbitrary", "arbitrary"), blocks, extra_bytes=scratch_bytes),
    )(proj, proj, proj, dt_raw, dtt, conv_w, conv_b.reshape(1, conv_dim), lane_vec(dt_bias), lane_vec(a_log),
      dt_bias.reshape(n_heads, 1), a_log.reshape(n_heads, 1),
      jnp.repeat(d_skip, SSD_HEAD_DIM).reshape(1, width), norm_w.reshape(1, width),
      *[rd.weight for rd in riders])
    return (out[0], out[1:]) if riders else out[0]


def _ln_kernel(*refs, alpha, emit_next):
    if emit_next:
        x_ref, br_ref, gate_ref, g_ref, b_ref, scale_ref, shift_ref, xo_ref, h_ref = refs
    else:
        x_ref, br_ref, gate_ref, g_ref, b_ref, xo_ref = refs
    v = alpha * x_ref[...] + gate_ref[...] * br_ref[...].astype(F32)
    mu = jnp.mean(v, axis=-1, keepdims=True)
    dev = v - mu
    var = jnp.mean(dev * dev, axis=-1, keepdims=True)
    y = dev * lax.rsqrt(var + LN_EPS) * g_ref[...] + b_ref[...]
    xo_ref[...] = y
    if emit_next:
        h_ref[...] = (y * (1.0 + scale_ref[...]) + shift_ref[...]).astype(h_ref.dtype)


def _residual_ln(x2, branch, gate, ln_g, ln_b, next_scale, next_shift, *, seq, alpha):
    m, d = x2.shape
    batch = m // seq
    ts = min(256, seq)
    n_s = seq // ts
    emit_next = next_scale is not None
    rows = pl.BlockSpec((ts, d), lambda b, i: (b * n_s + i, 0))
    per_batch = pl.BlockSpec((None, 1, d), lambda b, i: (b, 0, 0))
    shared = pl.BlockSpec((1, d), lambda b, i: (0, 0))
    in_specs = [rows, rows, per_batch, shared, shared]
    args = [x2, branch, gate, ln_g.reshape(1, d), ln_b.reshape(1, d)]
    out_specs = [rows]
    out_shape = [jax.ShapeDtypeStruct((m, d), F32)]
    blocks = [((ts, d), F32), ((ts, d), branch.dtype), ((ts, d), F32)]
    if emit_next:
        in_specs += [per_batch, per_batch]
        args += [next_scale, next_shift]
        out_specs.append(rows)
        out_shape.append(jax.ShapeDtypeStruct((m, d), BF16))
        blocks.append(((ts, d), BF16))
    out = pl.pallas_call(
        functools.partial(_ln_kernel, alpha=alpha, emit_next=emit_next),
        grid=(batch, n_s),
        in_specs=in_specs,
        out_specs=out_specs,
        out_shape=out_shape,
        name="residual_ln",
        compiler_params=_params(("parallel", "parallel"), blocks, extra_bytes=2 * _nbytes((ts, d), F32)),
    )(*args)
    return (out[0], out[1]) if emit_next else (out[0], None)


def _ffn_up_kernel(h_ref, wg_ref, wu_ref, cw_ref, cb_ref, o_ref, carry_ref, *, tm, rc, tiles_per_seq):
    i = pl.program_id(0)
    j = pl.program_id(1)
    tn = o_ref.shape[1]
    tail = jnp.where(i % tiles_per_seq == 0, 0.0, carry_ref[j])
    ridx = lax.broadcasted_iota(jnp.int32, (rc, tn), 0)
    for r in range(tm // rc):
        rows = slice(r * rc, (r + 1) * rc)
        gate = jnp.dot(h_ref[rows, :], wg_ref[...], preferred_element_type=F32)
        up = jnp.dot(h_ref[rows, :], wu_ref[...], preferred_element_type=F32)
        p1 = tail[V7X_SUBLANES - 1:V7X_SUBLANES]
        p2 = tail[V7X_SUBLANES - 2:V7X_SUBLANES - 1]
        g1 = jnp.where(ridx == 0, p1, pltpu.roll(gate, 1, 0))
        g2 = jnp.where(ridx == 0, p2, jnp.where(ridx == 1, p1, pltpu.roll(gate, 2, 0)))
        y = cb_ref[...] + cw_ref[0:1, :] * g2 + cw_ref[1:2, :] * g1 + cw_ref[2:3, :] * gate
        o_ref[rows, :] = (_silu(y) * up).astype(o_ref.dtype)
        tail = gate[rc - V7X_SUBLANES:, :]
    carry_ref[j] = tail


def _ffn_up(h, w_gate, w_up, layer, conv_w, conv_b, *, seq, tn):
    m, d = h.shape
    f = w_gate.shape[2]
    tm = min(2048, seq)
    rc = min(512, tm)
    n_j = f // tn
    blocks = [((tm, d), BF16), ((d, tn), BF16), ((d, tn), BF16), ((tm, tn), BF16)]
    w_spec = pl.BlockSpec((None, d, tn), lambda i, j: (layer, 0, j))
    return pl.pallas_call(
        functools.partial(_ffn_up_kernel, tm=tm, rc=rc, tiles_per_seq=seq // tm),
        grid=(m // tm, n_j),
        in_specs=[pl.BlockSpec((tm, d), lambda i, j: (i, 0)), w_spec, w_spec,
                  pl.BlockSpec((FFN_CONV, tn), lambda i, j: (0, j)),
                  pl.BlockSpec((1, tn), lambda i, j: (0, j))],
        out_specs=pl.BlockSpec((tm, tn), lambda i, j: (i, j)),
        out_shape=jax.ShapeDtypeStruct((m, f), BF16),
        name="ffn_up",
        scratch_shapes=[pltpu.VMEM((n_j, V7X_SUBLANES, tn), F32)],
        compiler_params=_params(("arbitrary", "arbitrary"), blocks,
                                extra_bytes=8 * _nbytes((rc, tn), F32)),
    )(h, w_gate, w_up, conv_w, conv_b.reshape(1, f))


def kernel(x, c, w_mod, b_mod, w_in, diff_lambda, diff_norm_w, ssd_conv_w, ssd_conv_b, ssd_dt_bias,
           ssd_a_log, ssd_d, ssd_norm_w, w_out, ln1_g, ln1_b, w_gate, w_up, ffn_conv_w, ffn_conv_b,
           w_down, ln2_g, ln2_b):
    batch, seq, d = x.shape
    depth = w_mod.shape[0]
    m = batch * seq
    n_ssd_heads = ssd_dt_bias.shape[1]
    ssd_width = n_ssd_heads * SSD_HEAD_DIM
    conv_dim = ssd_conv_w.shape[2]
    n_in = w_in.shape[2]
    att_width = (n_in - n_ssd_heads - conv_dim - ssd_width) // 3
    n_att_heads = att_width // ATT_VALUE_DIM
    n_main = n_in - n_ssd_heads
    assert att_width == ssd_width and conv_dim == 2 * ssd_width
    alpha = (2.0 * depth) ** 0.25
    ffn_tn = 256

    mod = _modulation(c, w_mod, b_mod)[:, :batch]
    mod = mod.reshape(depth, batch, 6, 1, d)
    shift1, scale1, gate1, shift2, scale2, gate2 = (mod[:, :, s] for s in range(6))

    def row_split(weight, layer):
        def make(grid):
            steps = int(np.prod(grid))
            rows, cols = weight.shape[1:]
            assert rows % (steps * V7X_BF16_SUBLANES) == 0

            def index(*ids):
                flat = ids[0]
                for extent, idx in zip(grid[1:], ids[1:]):
                    flat = flat * extent + idx
                return flat, 0
            return _CastRider(weight, layer, (rows // steps, cols), index)
        return make

    def tile_split(weight, layer):
        def make(grid):
            n_b, n_t = grid
            rows, cols = weight.shape[1:]
            assert rows % (n_t * V7X_BF16_SUBLANES) == 0 and cols % (n_b * V7X_LANES) == 0
            return _CastRider(weight, layer, (rows // n_t, cols // n_b), lambda b, t: (t, b))
        return make

    x2 = x.reshape(m, d)
    h = _modulate(x2, scale1[0], shift1[0], seq)
    w_in_l = w_in[:1].astype(BF16)
    for l in range(depth):
        last = l == depth - 1
        w_dt = jnp.pad(w_in_l[:, :, n_main:], ((0, 0), (0, 0), (0, V7X_LANES - n_ssd_heads)))
        proj = _matmul(h, w_in_l, 0, tm=1024, tn=1024, out_dtype=BF16, name="in_proj", n_cols=n_main)
        dt_raw = _matmul(h, w_dt, 0, tm=1024, tn=V7X_LANES, out_dtype=F32, name="dt_proj")
        lam_init = 0.8 - 0.6 * float(np.exp(-0.3 * l))
        y_att, (w_gate_l, w_up_l) = _attention(
            proj, diff_lambda[l], diff_norm_w[l], seq=seq, n_heads=n_att_heads, lam_init=lam_init,
            make_riders=lambda grid: [row_split(w_gate, l)(grid), row_split(w_up, l)(grid)])
        y_ssd, (w_down_l, w_out_l) = _ssd(
            proj, dt_raw, ssd_conv_w[l], ssd_conv_b[l], ssd_dt_bias[l], ssd_a_log[l], ssd_d[l],
            ssd_norm_w[l], seq=seq, z_col=3 * att_width // ssd_width,
            xbc_col=(3 * att_width + ssd_width) // conv_dim,
            make_riders=lambda grid: [tile_split(w_down, l)(grid), tile_split(w_out, l)(grid)])
        mix = _matmul_cat(y_att, y_ssd, w_out_l, 0, tm=1024, tn=1024, out_dtype=BF16)
        x2, h = _residual_ln(x2, mix, gate1[l], ln1_g[l], ln1_b[l], scale2[l], shift2[l],
                             seq=seq, alpha=alpha)
        act = _ffn_up(h, w_gate_l, w_up_l, 0, ffn_conv_w[l], ffn_conv_b[l], seq=seq, tn=ffn_tn)
        if last:
            ffn = _matmul(act, w_down_l, 0, tm=512, tn=512, out_dtype=BF16, name="ffn_down")
        else:
            ffn, (w_in_l,) = _matmul(act, w_down_l, 0, tm=512, tn=512, out_dtype=BF16, name="ffn_down",
                                     make_riders=lambda grid: [row_split(w_in, l + 1)(grid)])
        x2, h = _residual_ln(x2, ffn, gate2[l], ln2_g[l], ln2_b[l],
                             None if last else scale1[l + 1], None if last else shift1[l + 1],
                             seq=seq, alpha=alpha)
    return x2.reshape(batch, seq, d)
```

```python
import functools
from typing import Callable, NamedTuple, Tuple

import numpy as np
import jax
import jax.numpy as jnp
from jax import lax
from jax.experimental import pallas as pl
from jax.experimental.pallas import tpu as pltpu

F32 = jnp.float32
BF16 = jnp.bfloat16

CHUNK = 64
ATT_HEAD_DIM = 128
ATT_VALUE_DIM = 2 * ATT_HEAD_DIM
SSD_HEAD_DIM = 64
SSD_GROUPS = 8
SSD_STATE = 128
SSD_CONV = 4
FFN_CONV = 3
LN_EPS = 1e-5
RMS_EPS = 1e-5

V7X_LANES = 128
V7X_SUBLANES = 8
V7X_BF16_SUBLANES = 16
V7X_VMEM_BYTES = 64 * 1024 * 1024
VMEM_REQUEST_CAP = V7X_VMEM_BYTES - 8 * 1024 * 1024

MASK_VALUE = -1e30
LOG2_E = 1.4426950408889634
ATT_TILE = 1024
ATT_SWEEP_KEYS = 1024
ATT_ROW_BLOCK = 128
ATT_DIAG_ROW_BLOCK = 256
DIAG_MASK_PER_SLOPE = -1e32


def _nbytes(shape, dtype):
    return int(np.prod(shape)) * jnp.dtype(dtype).itemsize


def _params(semantics, blocks, extra_bytes=0):
    need = 2 * sum(_nbytes(s, d) for s, d in blocks) + extra_bytes + (4 << 20)
    return pltpu.CompilerParams(dimension_semantics=semantics,
                                vmem_limit_bytes=int(min(max(need, 16 << 20), VMEM_REQUEST_CAP)))


def _silu(v):
    u = 0.5 * v
    return u * jnp.tanh(u) + u


def _softplus(v):
    return jnp.maximum(v, 0.0) + jnp.log(1.0 + jnp.exp(-jnp.abs(v)))


class _CastRider(NamedTuple):
    weight: jax.Array
    layer: int
    block: Tuple[int, int]
    index: Callable


def _rider_specs(riders, n_trailing_index_args=0):
    in_specs, out_specs, out_shapes = [], [], []
    for rd in riders:
        def idx(*g, rd=rd, lead=None):
            g = g[:len(g) - n_trailing_index_args] if n_trailing_index_args else g
            return (lead,) + tuple(rd.index(*g))
        in_specs.append(pl.BlockSpec((None,) + rd.block, functools.partial(idx, lead=rd.layer)))
        out_specs.append(pl.BlockSpec((None,) + rd.block, functools.partial(idx, lead=0)))
        out_shapes.append(jax.ShapeDtypeStruct((1,) + rd.weight.shape[1:], BF16))
    return in_specs, out_specs, out_shapes


def _rider_blocks(riders):
    return [(rd.block, F32) for rd in riders] + [(rd.block, BF16) for rd in riders]


def _with_riders(body, n_leading, n_out, n_riders):
    if not n_riders:
        return body

    def wrapped(*refs):
        rider_in = refs[n_leading:n_leading + n_riders]
        host_out = refs[n_leading + n_riders:n_leading + n_riders + n_out]
        rider_out = refs[n_leading + n_riders + n_out:n_leading + 2 * n_riders + n_out]
        for src, dst in zip(rider_in, rider_out):
            dst[...] = src[...].astype(dst.dtype)
        body(*refs[:n_leading], *host_out, *refs[n_leading + 2 * n_riders + n_out:])

    return wrapped


def _mod_kernel(c_ref, w_ref, b_ref, o_ref):
    @pl.when(pl.program_id(2) == 0)
    def _():
        o_ref[...] = jnp.broadcast_to(b_ref[...], o_ref.shape)

    c_act = _silu(c_ref[...]).astype(BF16)
    o_ref[...] += jnp.dot(c_act, w_ref[...].astype(BF16), preferred_element_type=F32)


def _modulation(c, w_mod, b_mod):
    depth, d, n = w_mod.shape
    rows = V7X_SUBLANES
    c_pad = jnp.zeros((rows, d), F32).at[:c.shape[0]].set(c)
    tk, tn = min(1024, d), min(2048, n)
    blocks = [((rows, tk), F32), ((tk, tn), F32), ((1, tn), F32), ((rows, tn), F32)]
    return pl.pallas_call(
        _mod_kernel,
        grid=(depth, n // tn, d // tk),
        in_specs=[pl.BlockSpec((rows, tk), lambda l, j, k: (0, k)),
                  pl.BlockSpec((None, tk, tn), lambda l, j, k: (l, k, j)),
                  pl.BlockSpec((None, 1, tn), lambda l, j, k: (l, 0, j))],
        out_specs=pl.BlockSpec((None, rows, tn), lambda l, j, k: (l, 0, j)),
        out_shape=jax.ShapeDtypeStruct((depth, rows, n), F32),
        name="adaln_mod",
        compiler_params=_params(("parallel", "parallel", "arbitrary"), blocks,
                                extra_bytes=_nbytes((tk, tn), BF16)),
    )(c_pad, w_mod, b_mod.reshape(depth, 1, n))


def _modulate_kernel(x_ref, scale_ref, shift_ref, h_ref):
    h_ref[...] = (x_ref[...] * (1.0 + scale_ref[...]) + shift_ref[...]).astype(h_ref.dtype)


def _modulate(x2, scale, shift, seq):
    m, d = x2.shape
    batch = m // seq
    ts = min(256, seq)
    n_s = seq // ts
    vec = pl.BlockSpec((None, 1, d), lambda b, i: (b, 0, 0))
    blocks = [((ts, d), F32), ((ts, d), BF16)]
    return pl.pallas_call(
        _modulate_kernel,
        grid=(batch, n_s),
        in_specs=[pl.BlockSpec((ts, d), lambda b, i: (b * n_s + i, 0)), vec, vec],
        out_specs=pl.BlockSpec((ts, d), lambda b, i: (b * n_s + i, 0)),
        out_shape=jax.ShapeDtypeStruct((m, d), BF16),
        name="modulate",
        compiler_params=_params(("parallel", "parallel"), blocks),
    )(x2, scale, shift)


def _matmul_kernel(a_ref, w_ref, o_ref):
    o_ref[...] = jnp.dot(a_ref[...], w_ref[...], preferred_element_type=F32).astype(o_ref.dtype)


def _matmul(a, w, layer, *, tm, tn, out_dtype, name, n_cols=None, make_riders=None):
    m, k = a.shape
    n = w.shape[2] if n_cols is None else n_cols
    tm, tn = min(tm, m), min(tn, n)
    riders = make_riders((m // tm, n // tn)) if make_riders else ()
    r_in, r_out, r_shapes = _rider_specs(riders)
    blocks = [((tm, k), a.dtype), ((k, tn), w.dtype), ((tm, tn), out_dtype)] + _rider_blocks(riders)
    out = pl.pallas_call(
        _with_riders(_matmul_kernel, 2, 1, len(riders)),
        grid=(m // tm, n // tn),
        in_specs=[pl.BlockSpec((tm, k), lambda i, j: (i, 0)),
                  pl.BlockSpec((None, k, tn), lambda i, j: (layer, 0, j))] + r_in,
        out_specs=[pl.BlockSpec((tm, tn), lambda i, j: (i, j))] + r_out,
        out_shape=[jax.ShapeDtypeStruct((m, n), out_dtype)] + r_shapes,
        name=name,
        compiler_params=_params(("arbitrary", "arbitrary"), blocks, extra_bytes=_nbytes((tm, tn), F32)),
    )(a, w, *[rd.weight for rd in riders])
    return (out[0], out[1:]) if riders else out[0]


def _matmul_cat_kernel(a0_ref, a1_ref, w0_ref, w1_ref, o_ref):
    acc = jnp.dot(a0_ref[...], w0_ref[...], preferred_element_type=F32)
    acc += jnp.dot(a1_ref[...], w1_ref[...], preferred_element_type=F32)
    o_ref[...] = acc.astype(o_ref.dtype)


def _matmul_cat(a0, a1, w, layer, *, tm, tn, out_dtype):
    m, k0 = a0.shape
    assert a1.shape == (m, k0) and w.shape[1] == 2 * k0
    n = w.shape[2]
    tm, tn = min(tm, m), min(tn, n)
    blocks = [((tm, 2 * k0), a0.dtype), ((2 * k0, tn), w.dtype), ((tm, tn), out_dtype)]
    return pl.pallas_call(
        _matmul_cat_kernel,
        grid=(m // tm, n // tn),
        in_specs=[pl.BlockSpec((tm, k0), lambda i, j: (i, 0)),
                  pl.BlockSpec((tm, k0), lambda i, j: (i, 0)),
                  pl.BlockSpec((None, k0, tn), lambda i, j: (layer, 0, j)),
                  pl.BlockSpec((None, k0, tn), lambda i, j: (layer, 1, j))],
        out_specs=pl.BlockSpec((tm, tn), lambda i, j: (i, j)),
        out_shape=jax.ShapeDtypeStruct((m, n), out_dtype),
        name="out_proj",
        compiler_params=_params(("parallel", "arbitrary"), blocks, extra_bytes=_nbytes((tm, tn), F32)),
    )(a0, a1, w, w)


def _attn_kernel(slopes_ref, q_ref, k_ref, v_ref, diag_ref, lam_ref, nw_ref, o_ref, qs_ref, m_ref, l_ref,
                 acc_ref, *, tile, rb, rb_diag, lam_init):
    h = pl.program_id(1)
    i = pl.program_id(2)
    slope = slopes_ref[h] * LOG2_E
    dh = ATT_HEAD_DIM
    nt = (((1,), (1,)), ((), ()))
    for mp in range(2):
        q_mp = q_ref[:, mp * dh:(mp + 1) * dh].astype(F32) * (dh ** -0.5 * LOG2_E)
        qs_ref[mp] = q_mp.astype(BF16)

    lanes = V7X_LANES

    def lane_tiled(x, width):
        return jnp.concatenate([x] * (width // lanes), axis=1)

    def update(rows, mp, s, tile_bias, v_t, first):
        nrows, kw = s.shape
        smax = jnp.broadcast_to(jnp.max(s, axis=-1, keepdims=True), (nrows, lanes)) + tile_bias
        if first:
            m_new = smax
        else:
            m_old = m_ref[mp, rows]
            m_new = jnp.maximum(m_old, smax)
            alpha = jnp.exp2(m_old - m_new)
        p = jnp.exp2(s - lane_tiled(m_new - tile_bias, kw))
        psum = p[:, 0:lanes]
        for c in range(1, kw // lanes):
            psum = psum + p[:, c * lanes:(c + 1) * lanes]
        pv = jnp.dot(p.astype(BF16), v_t, preferred_element_type=F32)
        if first:
            l_ref[mp, rows] = psum
            acc_ref[mp, rows] = pv
        else:
            l_ref[mp, rows] = alpha * l_ref[mp, rows] + psum
            acc_ref[mp, rows] = lane_tiled(alpha, pv.shape[1]) * acc_ref[mp, rows] + pv
        m_ref[mp, rows] = m_new

    k_diag = pl.multiple_of(i * tile, tile)
    diag_bias = slope * (i * tile).astype(F32)
    for r in range(tile // rb_diag):
        rows = slice(r * rb_diag, (r + 1) * rb_diag)
        kw = (r + 1) * rb_diag
        bias = slope * diag_ref[rows, 0:kw]
        v_t = v_ref[pl.ds(k_diag, kw), :]
        for mp in range(2):
            kt = k_ref[pl.ds(k_diag, kw), mp * dh:(mp + 1) * dh]
            s = lax.dot_general(qs_ref[mp, rows], kt, nt, preferred_element_type=F32) + bias
            update(rows, mp, s, diag_bias, v_t, first=True)

    sweep = max(tile, ATT_SWEEP_KEYS)
    tiles_per_sweep = sweep // tile
    assert tiles_per_sweep in (1, 2)
    col_bias = slope * lax.broadcasted_iota(jnp.int32, (1, sweep), 1).astype(F32)

    def visible_keys(k0, kw):
        tile_bias = slope * k0.astype(F32)
        v_t = v_ref[pl.ds(k0, kw), :]
        for r in range(tile // rb):
            rows = slice(r * rb, (r + 1) * rb)
            for mp in range(2):
                kt = k_ref[pl.ds(k0, kw), mp * dh:(mp + 1) * dh]
                s = lax.dot_general(qs_ref[mp, rows], kt, nt, preferred_element_type=F32)
                update(rows, mp, s + col_bias[:, :kw], tile_bias, v_t, first=False)

    def body(jj, carry):
        visible_keys(pl.multiple_of(jj * sweep, sweep), sweep)
        return carry

    lax.fori_loop(0, i // tiles_per_sweep, body, 0)

    if tiles_per_sweep == 2:
        @pl.when(i % 2 == 1)
        def _():
            visible_keys(pl.multiple_of((i - 1) * tile, tile), tile)

    lp = lam_ref[...]
    lam = (jnp.exp(jnp.sum(lp[0:1] * lp[1:2], axis=-1, keepdims=True))
           - jnp.exp(jnp.sum(lp[2:3] * lp[3:4], axis=-1, keepdims=True)) + lam_init)
    inv_l = [1.0 / jnp.sum(l_ref[mp], axis=-1, keepdims=True) for mp in range(2)]
    o = acc_ref[0] * inv_l[0] - lam * (acc_ref[1] * inv_l[1])
    ms = jnp.mean(o * o, axis=-1, keepdims=True)
    y = o * lax.rsqrt(ms + RMS_EPS) * nw_ref[...] * (1.0 - lam_init)
    o_ref[...] = y.astype(o_ref.dtype)


def _attention(proj, lam_params, norm_w, *, seq, n_heads, lam_init, make_riders=None):
    m = proj.shape[0]
    batch = m // seq
    tile = min(ATT_TILE, seq)
    n_q = seq // tile
    dv = ATT_VALUE_DIM
    slopes = jnp.asarray(np.array([2.0 ** (-8.0 * (hh + 1) / n_heads) for hh in range(n_heads)],
                                  dtype=np.float32))
    tq = np.arange(tile)[:, None]
    tk = np.arange(tile)[None, :]
    diag_table = jnp.asarray(np.where(tk // CHUNK <= tq // CHUNK, tq - np.abs(tq - tk), DIAG_MASK_PER_SLOPE)
                             .astype(np.float32))
    riders = make_riders((batch, n_heads, n_q)) if make_riders else ()
    r_in, r_out, r_shapes = _rider_specs(riders, n_trailing_index_args=1)
    blocks = [((tile, dv), BF16), ((seq, dv), BF16), ((seq, dv), BF16), ((tile, tile), F32),
              ((tile, dv), BF16)] + _rider_blocks(riders)
    scratch_bytes = (2 * _nbytes((2, tile, V7X_LANES), F32) + _nbytes((2, tile, dv), F32)
                     + 6 * _nbytes((tile, tile), F32))
    grid_spec = pltpu.PrefetchScalarGridSpec(
        num_scalar_prefetch=1,
        grid=(batch, n_heads, n_q),
        in_specs=[pl.BlockSpec((tile, dv), lambda b, h, i, s: (b * n_q + i, h)),
                  pl.BlockSpec((seq, dv), lambda b, h, i, s: (b, n_heads + h)),
                  pl.BlockSpec((seq, dv), lambda b, h, i, s: (b, 2 * n_heads + h)),
                  pl.BlockSpec((tile, tile), lambda b, h, i, s: (0, 0)),
                  pl.BlockSpec((4, ATT_HEAD_DIM), lambda b, h, i, s: (0, 0)),
                  pl.BlockSpec((1, dv), lambda b, h, i, s: (0, 0))] + r_in,
        out_specs=[pl.BlockSpec((tile, dv), lambda b, h, i, s: (b * n_q + i, h))] + r_out,
        scratch_shapes=[pltpu.VMEM((2, tile, ATT_HEAD_DIM), BF16), pltpu.VMEM((2, tile, V7X_LANES), F32),
                        pltpu.VMEM((2, tile, V7X_LANES), F32), pltpu.VMEM((2, tile, dv), F32)],
    )
    body = functools.partial(_attn_kernel, tile=tile, rb=min(ATT_ROW_BLOCK, tile),
                             rb_diag=min(ATT_DIAG_ROW_BLOCK, tile), lam_init=lam_init)
    out = pl.pallas_call(
        _with_riders(body, 7, 1, len(riders)),
        grid_spec=grid_spec,
        out_shape=[jax.ShapeDtypeStruct((m, n_heads * dv), BF16)] + r_shapes,
        name="diff_attention",
        compiler_params=_params(("arbitrary", "arbitrary", "arbitrary"), blocks, extra_bytes=scratch_bytes),
    )(slopes, proj, proj, proj, diag_table, lam_params, norm_w.reshape(1, dv), *[rd.weight for rd in riders])
    return (out[0], out[1:]) if riders else out[0]


def _ssd_kernel(z_ref, xbc_ref, halo_ref, dt_ref, dtt_ref, cw_ref, cb_ref, dtb_l_ref, alog_l_ref,
                dtb_s_ref, alog_s_ref, d_ref, nw_ref, o_ref, state_ref, xe_ref, *, tile, width):
    t = pl.program_id(1)
    gw = width // SSD_GROUPS
    hpg = gw // SSD_HEAD_DIM
    n = SSD_STATE
    b_off, c_off = width, width + SSD_GROUPS * n

    @pl.when(t == 0)
    def _():
        state_ref[...] = jnp.zeros_like(state_ref)

    row = lax.broadcasted_iota(jnp.int32, (tile, tile), 0)
    col = lax.broadcasted_iota(jnp.int32, (tile, tile), 1)
    causal = row >= col
    dt_s = _softplus(dt_ref[...] + dtb_l_ref[...])
    da_s = dt_s * (-jnp.exp(alog_l_ref[...]))
    acs_s = jnp.dot(causal.astype(F32), da_s, precision=lax.Precision.HIGHEST,
                    preferred_element_type=F32)
    dt_l = _softplus(dtt_ref[...] + dtb_s_ref[...])
    da_l = dt_l * (-jnp.exp(alog_s_ref[...]))
    acs_l = jnp.dot(da_l, (row <= col).astype(F32), precision=lax.Precision.HIGHEST,
                    preferred_element_type=F32)

    head_of_lane = lax.broadcasted_iota(jnp.int32, (1, gw), 1) // SSD_HEAD_DIM

    def per_head(vals):
        out = vals[hpg - 1]
        for r in range(hpg - 2, -1, -1):
            out = jnp.where(head_of_lane == r, vals[r], out)
        return out

    halo_rows = halo_ref.shape[0]
    xe_ref[0:halo_rows, :] = jnp.where(t > 0, halo_ref[...].astype(F32), 0.0)
    xe_ref[halo_rows:, :] = xbc_ref[...].astype(F32)

    def conv_silu(c0, w):
        y = cb_ref[:, c0:c0 + w]
        for j in range(SSD_CONV):
            off = halo_rows - (SSD_CONV - 1) + j
            y = y + cw_ref[j:j + 1, c0:c0 + w] * xe_ref[off:off + tile, c0:c0 + w]
        return _silu(y)

    for g in range(SSD_GROUPS):
        xs = conv_silu(g * gw, gw)
        bm = conv_silu(b_off + g * n, n).astype(BF16)
        cm = conv_silu(c_off + g * n, n).astype(BF16)
        heads = [g * hpg + r for r in range(hpg)]
        acs_cols = [acs_s[:, hh:hh + 1] for hh in heads]
        xdt = xs * per_head([dt_s[:, hh:hh + 1] for hh in heads])
        cb = lax.dot_general(cm, bm, (((1,), (1,)), ((), ())), preferred_element_type=F32)
        prev = state_ref[g]
        y = jnp.dot(cm, prev.astype(BF16), preferred_element_type=F32)
        y = y * per_head([jnp.exp(a) for a in acs_cols])
        for r, hh in enumerate(heads):
            seg = acs_cols[r] - acs_l[hh:hh + 1, :]
            decay = jnp.exp(jnp.where(causal, seg, MASK_VALUE))
            x_r = jnp.where(head_of_lane == r, xdt, 0.0).astype(BF16)
            y = y + jnp.dot((cb * decay).astype(BF16), x_r, preferred_element_type=F32)
        last = [a[tile - 1:tile, :] for a in acs_cols]
        to_end = per_head([jnp.exp(last[r] - acs_cols[r]) for r in range(hpg)])
        new_state = lax.dot_general(bm, (xdt * to_end).astype(BF16), (((0,), (0,)), ((), ())),
                                    preferred_element_type=F32)
        state_ref[g] = prev * per_head([jnp.exp(a) for a in last]) + new_state
        lanes = slice(g * gw, (g + 1) * gw)
        y = y + d_ref[:, lanes] * xs
        y = y * _silu(z_ref[:, lanes].astype(F32))
        y = y * lax.rsqrt(jnp.mean(y * y, axis=-1, keepdims=True) + RMS_EPS) * nw_ref[:, lanes]
        o_ref[:, lanes] = y.astype(o_ref.dtype)


def _ssd(proj, dt_raw, conv_w, conv_b, dt_bias, a_log, d_skip, norm_w, *, seq, z_col, xbc_col, make_riders=None):
    m = proj.shape[0]
    batch = m // seq
    n_heads = dt_bias.shape[0]
    width = n_heads * SSD_HEAD_DIM
    conv_dim = conv_w.shape[1]
    tile = min(256, seq)
    n_t = seq // tile
    halo = V7X_BF16_SUBLANES
    halo_per_tile = tile // halo
    lanes = dt_raw.shape[1]

    def lane_vec(v):
        return jnp.zeros((1, lanes), F32).at[0, :n_heads].set(v)

    dtt = dt_raw[:, :n_heads].T
    const = lambda b, t: (0, 0)
    riders = make_riders((batch, n_t)) if make_riders else ()
    r_in, r_out, r_shapes = _rider_specs(riders)
    blocks = [((tile, width), BF16), ((tile, conv_dim), BF16), ((halo, conv_dim), BF16),
              ((tile, lanes), F32), ((n_heads, tile), F32), ((tile, width), BF16)] + _rider_blocks(riders)
    scratch_bytes = _nbytes((SSD_GROUPS, SSD_STATE, width // SSD_GROUPS), F32) + 12 * _nbytes((tile, tile), F32) \
        + 2 * _nbytes((tile + halo, conv_dim), F32)
    out = pl.pallas_call(
        _with_riders(functools.partial(_ssd_kernel, tile=tile, width=width), 13, 1, len(riders)),
        grid=(batch, n_t),
        in_specs=[pl.BlockSpec((tile, width), lambda b, t: (b * n_t + t, z_col)),
                  pl.BlockSpec((tile, conv_dim), lambda b, t: (b * n_t + t, xbc_col)),
                  pl.BlockSpec((halo, conv_dim),
                               lambda b, t: (jnp.maximum((b * n_t + t) * halo_per_tile - 1, 0), xbc_col)),
                  pl.BlockSpec((tile, lanes), lambda b, t: (b * n_t + t, 0)),
                  pl.BlockSpec((n_heads, tile), lambda b, t: (0, b * n_t + t)),
                  pl.BlockSpec((SSD_CONV, conv_dim), const),
                  pl.BlockSpec((1, conv_dim), const),
                  pl.BlockSpec((1, lanes), const),
                  pl.BlockSpec((1, lanes), const),
                  pl.BlockSpec((n_heads, 1), const),
                  pl.BlockSpec((n_heads, 1), const),
                  pl.BlockSpec((1, width), const),
                  pl.BlockSpec((1, width), const)] + r_in,
        out_specs=[pl.BlockSpec((tile, width), lambda b, t: (b * n_t + t, 0))] + r_out,
        out_shape=[jax.ShapeDtypeStruct((m, width), BF16)] + r_shapes,
        name="ssd_mixer",
        scratch_shapes=[pltpu.VMEM((SSD_GROUPS, SSD_STATE, width // SSD_GROUPS), F32),
                        pltpu.VMEM((tile + halo, conv_dim), F32)],
        compiler_params=_params(("arbitrary", "arbitrary"), blocks, extra_bytes=scratch_bytes),
    )(proj, proj, proj, dt_raw, dtt, conv_w, conv_b.reshape(1, conv_dim), lane_vec(dt_bias), lane_vec(a_log),
      dt_bias.reshape(n_heads, 1), a_log.reshape(n_heads, 1),
      jnp.repeat(d_skip, SSD_HEAD_DIM).reshape(1, width), norm_w.reshape(1, width),
      *[rd.weight for rd in riders])
    return (out[0], out[1:]) if riders else out[0]


def _ln_kernel(*refs, alpha, emit_next):
    if emit_next:
        x_ref, br_ref, gate_ref, g_ref, b_ref, scale_ref, shift_ref, xo_ref, h_ref = refs
    else:
        x_ref, br_ref, gate_ref, g_ref, b_ref, xo_ref = refs
    v = alpha * x_ref[...] + gate_ref[...] * br_ref[...].astype(F32)
    mu = jnp.mean(v, axis=-1, keepdims=True)
    dev = v - mu
    var = jnp.mean(dev * dev, axis=-1, keepdims=True)
    y = dev * lax.rsqrt(var + LN_EPS) * g_ref[...] + b_ref[...]
    xo_ref[...] = y
    if emit_next:
        h_ref[...] = (y * (1.0 + scale_ref[...]) + shift_ref[...]).astype(h_ref.dtype)


def _residual_ln(x2, branch, gate, ln_g, ln_b, next_scale, next_shift, *, seq, alpha):
    m, d = x2.shape
    batch = m // seq
    ts = min(256, seq)
    n_s = seq // ts
    emit_next = next_scale is not None
    rows = pl.BlockSpec((ts, d), lambda b, i: (b * n_s + i, 0))
    per_batch = pl.BlockSpec((None, 1, d), lambda b, i: (b, 0, 0))
    shared = pl.BlockSpec((1, d), lambda b, i: (0, 0))
    in_specs = [rows, rows, per_batch, shared, shared]
    args = [x2, branch, gate, ln_g.reshape(1, d), ln_b.reshape(1, d)]
    out_specs = [rows]
    out_shape = [jax.ShapeDtypeStruct((m, d), F32)]
    blocks = [((ts, d), F32), ((ts, d), branch.dtype), ((ts, d), F32)]
    if emit_next:
        in_specs += [per_batch, per_batch]
        args += [next_scale, next_shift]
        out_specs.append(rows)
        out_shape.append(jax.ShapeDtypeStruct((m, d), BF16))
        blocks.append(((ts, d), BF16))
    out = pl.pallas_call(
        functools.partial(_ln_kernel, alpha=alpha, emit_next=emit_next),
        grid=(batch, n_s),
        in_specs=in_specs,
        out_specs=out_specs,
        out_shape=out_shape,
        name="residual_ln",
        compiler_params=_params(("parallel", "parallel"), blocks, extra_bytes=2 * _nbytes((ts, d), F32)),
    )(*args)
    return (out[0], out[1]) if emit_next else (out[0], None)


def _ffn_up_kernel(h_ref, wg_ref, wu_ref, cw_ref, cb_ref, o_ref, carry_ref, *, tm, rc, tiles_per_seq):
    i = pl.program_id(0)
    j = pl.program_id(1)
    tn = o_ref.shape[1]
    tail = jnp.where(i % tiles_per_seq == 0, 0.0, carry_ref[j])
    ridx = lax.broadcasted_iota(jnp.int32, (rc, tn), 0)
    for r in range(tm // rc):
        rows = slice(r * rc, (r + 1) * rc)
        gate = jnp.dot(h_ref[rows, :], wg_ref[...], preferred_element_type=F32)
        up = jnp.dot(h_ref[rows, :], wu_ref[...], preferred_element_type=F32)
        p1 = tail[V7X_SUBLANES - 1:V7X_SUBLANES]
        p2 = tail[V7X_SUBLANES - 2:V7X_SUBLANES - 1]
        g1 = jnp.where(ridx == 0, p1, pltpu.roll(gate, 1, 0))
        g2 = jnp.where(ridx == 0, p2, jnp.where(ridx == 1, p1, pltpu.roll(gate, 2, 0)))
        y = cb_ref[...] + cw_ref[0:1, :] * g2 + cw_ref[1:2, :] * g1 + cw_ref[2:3, :] * gate
        o_ref[rows, :] = (_silu(y) * up).astype(o_ref.dtype)
        tail = gate[rc - V7X_SUBLANES:, :]
    carry_ref[j] = tail


def _ffn_up(h, w_gate, w_up, layer, conv_w, conv_b, *, seq, tn):
    m, d = h.shape
    f = w_gate.shape[2]
    tm = min(2048, seq)
    rc = min(512, tm)
    n_j = f // tn
    blocks = [((tm, d), BF16), ((d, tn), BF16), ((d, tn), BF16), ((tm, tn), BF16)]
    w_spec = pl.BlockSpec((None, d, tn), lambda i, j: (layer, 0, j))
    return pl.pallas_call(
        functools.partial(_ffn_up_kernel, tm=tm, rc=rc, tiles_per_seq=seq // tm),
        grid=(m // tm, n_j),
        in_specs=[pl.BlockSpec((tm, d), lambda i, j: (i, 0)), w_spec, w_spec,
                  pl.BlockSpec((FFN_CONV, tn), lambda i, j: (0, j)),
                  pl.BlockSpec((1, tn), lambda i, j: (0, j))],
        out_specs=pl.BlockSpec((tm, tn), lambda i, j: (i, j)),
        out_shape=jax.ShapeDtypeStruct((m, f), BF16),
        name="ffn_up",
        scratch_shapes=[pltpu.VMEM((n_j, V7X_SUBLANES, tn), F32)],
        compiler_params=_params(("arbitrary", "arbitrary"), blocks,
                                extra_bytes=8 * _nbytes((rc, tn), F32)),
    )(h, w_gate, w_up, conv_w, conv_b.reshape(1, f))


def kernel(x, c, w_mod, b_mod, w_in, diff_lambda, diff_norm_w, ssd_conv_w, ssd_conv_b, ssd_dt_bias,
           ssd_a_log, ssd_d, ssd_norm_w, w_out, ln1_g, ln1_b, w_gate, w_up, ffn_conv_w, ffn_conv_b,
           w_down, ln2_g, ln2_b):
    batch, seq, d = x.shape
    depth = w_mod.shape[0]
    m = batch * seq
    n_ssd_heads = ssd_dt_bias.shape[1]
    ssd_width = n_ssd_heads * SSD_HEAD_DIM
    conv_dim = ssd_conv_w.shape[2]
    n_in = w_in.shape[2]
    att_width = (n_in - n_ssd_heads - conv_dim - ssd_width) // 3
    n_att_heads = att_width // ATT_VALUE_DIM
    n_main = n_in - n_ssd_heads
    assert att_width == ssd_width and conv_dim == 2 * ssd_width
    alpha = (2.0 * depth) ** 0.25
    ffn_tn = 256

    mod = _modulation(c, w_mod, b_mod)[:, :batch]
    mod = mod.reshape(depth, batch, 6, 1, d)
    shift1, scale1, gate1, shift2, scale2, gate2 = (mod[:, :, s] for s in range(6))

    def row_split(weight, layer):
        def make(grid):
            steps = int(np.prod(grid))
            rows, cols = weight.shape[1:]
            assert rows % (steps * V7X_BF16_SUBLANES) == 0

            def index(*ids):
                flat = ids[0]
                for extent, idx in zip(grid[1:], ids[1:]):
                    flat = flat * extent + idx
                return flat, 0
            return _CastRider(weight, layer, (rows // steps, cols), index)
        return make

    def tile_split(weight, layer):
        def make(grid):
            n_b, n_t = grid
            rows, cols = weight.shape[1:]
            assert rows % (n_t * V7X_BF16_SUBLANES) == 0 and cols % (n_b * V7X_LANES) == 0
            return _CastRider(weight, layer, (rows // n_t, cols // n_b), lambda b, t: (t, b))
        return make

    x2 = x.reshape(m, d)
    h = _modulate(x2, scale1[0], shift1[0], seq)
    w_in_bf = w_in.astype(BF16)
    w_dt = jnp.pad(w_in_bf[:, :, n_main:], ((0, 0), (0, 0), (0, V7X_LANES - n_ssd_heads)))
    for l in range(depth):
        last = l == depth - 1
        proj = _matmul(h, w_in_bf, l, tm=1024, tn=1024, out_dtype=BF16, name="in_proj", n_cols=n_main)
        dt_raw = _matmul(h, w_dt, l, tm=1024, tn=V7X_LANES, out_dtype=F32, name="dt_proj")
        lam_init = 0.8 - 0.6 * float(np.exp(-0.3 * l))
        y_att, (w_gate_l, w_up_l) = _attention(
            proj, diff_lambda[l], diff_norm_w[l], seq=seq, n_heads=n_att_heads, lam_init=lam_init,
            make_riders=lambda grid: [row_split(w_gate, l)(grid), row_split(w_up, l)(grid)])
        y_ssd, (w_down_l, w_out_l) = _ssd(
            proj, dt_raw, ssd_conv_w[l], ssd_conv_b[l], ssd_dt_bias[l], ssd_a_log[l], ssd_d[l],
            ssd_norm_w[l], seq=seq, z_col=3 * att_width // ssd_width,
            xbc_col=(3 * att_width + ssd_width) // conv_dim,
            make_riders=lambda grid: [tile_split(w_down, l)(grid), tile_split(w_out, l)(grid)])
        mix = _matmul_cat(y_att, y_ssd, w_out_l, 0, tm=1024, tn=1024, out_dtype=BF16)
        x2, h = _residual_ln(x2, mix, gate1[l], ln1_g[l], ln1_b[l], scale2[l], shift2[l],
                             seq=seq, alpha=alpha)
        act = _ffn_up(h, w_gate_l, w_up_l, 0, ffn_conv_w[l], ffn_conv_b[l], seq=seq, tn=ffn_tn)
        ffn = _matmul(act, w_down_l, 0, tm=512, tn=512, out_dtype=BF16, name="ffn_down")
        x2, h = _residual_ln(x2, ffn, gate2[l], ln2_g[l], ln2_b[l],
                             None if last else scale1[l + 1], None if last else shift1[l + 1],
                             seq=seq, alpha=alpha)
    return x2.reshape(batch, seq, d)
```

```python
import functools
from typing import Callable, NamedTuple, Tuple

import numpy as np
import jax
import jax.numpy as jnp
from jax import lax
from jax.experimental import pallas as pl
from jax.experimental.pallas import tpu as pltpu

F32 = jnp.float32
BF16 = jnp.bfloat16

CHUNK = 64
ATT_HEAD_DIM = 128
ATT_VALUE_DIM = 2 * ATT_HEAD_DIM
SSD_HEAD_DIM = 64
SSD_GROUPS = 8
SSD_STATE = 128
SSD_CONV = 4
FFN_CONV = 3
LN_EPS = 1e-5
RMS_EPS = 1e-5

V7X_LANES = 128
V7X_SUBLANES = 8
V7X_BF16_SUBLANES = 16
V7X_VMEM_BYTES = 64 * 1024 * 1024
VMEM_REQUEST_CAP = V7X_VMEM_BYTES - 8 * 1024 * 1024

MASK_VALUE = -1e30
LOG2_E = 1.4426950408889634
NT_DIMS = (((1,), (1,)), ((), ()))
ATT_TILE = 1024
ATT_SWEEP_KEYS = 1024
ATT_ROW_BLOCK = 128
ATT_DIAG_ROW_BLOCK = 256
DIAG_MASK_PER_SLOPE = -1e32


def _nbytes(shape, dtype):
    return int(np.prod(shape)) * jnp.dtype(dtype).itemsize


def _params(semantics, blocks, extra_bytes=0):
    need = 2 * sum(_nbytes(s, d) for s, d in blocks) + extra_bytes + (4 << 20)
    return pltpu.CompilerParams(dimension_semantics=semantics,
                                vmem_limit_bytes=int(min(max(need, 16 << 20), VMEM_REQUEST_CAP)))


def _silu(v):
    u = 0.5 * v
    return u * jnp.tanh(u) + u


def _softplus(v):
    return jnp.maximum(v, 0.0) + jnp.log(1.0 + jnp.exp(-jnp.abs(v)))


class _CastRider(NamedTuple):
    weight: jax.Array
    layer: int
    block: Tuple[int, int]
    index: Callable


def _rider_specs(riders, n_trailing_index_args=0):
    in_specs, out_specs, out_shapes = [], [], []
    for rd in riders:
        def idx(*g, rd=rd, lead=None):
            g = g[:len(g) - n_trailing_index_args] if n_trailing_index_args else g
            return (lead,) + tuple(rd.index(*g))
        in_specs.append(pl.BlockSpec((None,) + rd.block, functools.partial(idx, lead=rd.layer)))
        out_specs.append(pl.BlockSpec((None,) + rd.block, functools.partial(idx, lead=0)))
        out_shapes.append(jax.ShapeDtypeStruct((1,) + rd.weight.shape[1:], BF16))
    return in_specs, out_specs, out_shapes


def _rider_blocks(riders):
    return [(rd.block, F32) for rd in riders] + [(rd.block, BF16) for rd in riders]


def _with_riders(body, n_leading, n_out, n_riders):
    if not n_riders:
        return body

    def wrapped(*refs):
        rider_in = refs[n_leading:n_leading + n_riders]
        host_out = refs[n_leading + n_riders:n_leading + n_riders + n_out]
        rider_out = refs[n_leading + n_riders + n_out:n_leading + 2 * n_riders + n_out]
        for src, dst in zip(rider_in, rider_out):
            dst[...] = src[...].astype(dst.dtype)
        body(*refs[:n_leading], *host_out, *refs[n_leading + 2 * n_riders + n_out:])

    return wrapped


def _mod_kernel(c_ref, w_ref, b_ref, o_ref):
    @pl.when(pl.program_id(2) == 0)
    def _():
        o_ref[...] = jnp.broadcast_to(b_ref[...], o_ref.shape)

    c_act = _silu(c_ref[...]).astype(BF16)
    o_ref[...] += jnp.dot(c_act, w_ref[...].astype(BF16), preferred_element_type=F32)


def _modulation(c, w_mod, b_mod):
    depth, d, n = w_mod.shape
    rows = V7X_SUBLANES
    c_pad = jnp.zeros((rows, d), F32).at[:c.shape[0]].set(c)
    tk, tn = min(1024, d), min(2048, n)
    blocks = [((rows, tk), F32), ((tk, tn), F32), ((1, tn), F32), ((rows, tn), F32)]
    return pl.pallas_call(
        _mod_kernel,
        grid=(depth, n // tn, d // tk),
        in_specs=[pl.BlockSpec((rows, tk), lambda l, j, k: (0, k)),
                  pl.BlockSpec((None, tk, tn), lambda l, j, k: (l, k, j)),
                  pl.BlockSpec((None, 1, tn), lambda l, j, k: (l, 0, j))],
        out_specs=pl.BlockSpec((None, rows, tn), lambda l, j, k: (l, 0, j)),
        out_shape=jax.ShapeDtypeStruct((depth, rows, n), F32),
        name="adaln_mod",
        compiler_params=_params(("parallel", "parallel", "arbitrary"), blocks,
                                extra_bytes=_nbytes((tk, tn), BF16)),
    )(c_pad, w_mod, b_mod.reshape(depth, 1, n))


def _modulate_kernel(x_ref, scale_ref, shift_ref, h_ref):
    h_ref[...] = (x_ref[...] * (1.0 + scale_ref[...]) + shift_ref[...]).astype(h_ref.dtype)


def _modulate(x2, scale, shift, seq):
    m, d = x2.shape
    batch = m // seq
    ts = min(256, seq)
    n_s = seq // ts
    vec = pl.BlockSpec((None, 1, d), lambda b, i: (b, 0, 0))
    blocks = [((ts, d), F32), ((ts, d), BF16)]
    return pl.pallas_call(
        _modulate_kernel,
        grid=(batch, n_s),
        in_specs=[pl.BlockSpec((ts, d), lambda b, i: (b * n_s + i, 0)), vec, vec],
        out_specs=pl.BlockSpec((ts, d), lambda b, i: (b * n_s + i, 0)),
        out_shape=jax.ShapeDtypeStruct((m, d), BF16),
        name="modulate",
        compiler_params=_params(("parallel", "parallel"), blocks),
    )(x2, scale, shift)


def _matmul_kernel(a_ref, w_ref, o_ref):
    o_ref[...] = jnp.dot(a_ref[...], w_ref[...], preferred_element_type=F32).astype(o_ref.dtype)


def _matmul(a, w, layer, *, tm, tn, out_dtype, name, n_cols=None, make_riders=None):
    m, k = a.shape
    n = w.shape[2] if n_cols is None else n_cols
    tm, tn = min(tm, m), min(tn, n)
    riders = make_riders((m // tm, n // tn)) if make_riders else ()
    r_in, r_out, r_shapes = _rider_specs(riders)
    blocks = [((tm, k), a.dtype), ((k, tn), w.dtype), ((tm, tn), out_dtype)] + _rider_blocks(riders)
    out = pl.pallas_call(
        _with_riders(_matmul_kernel, 2, 1, len(riders)),
        grid=(m // tm, n // tn),
        in_specs=[pl.BlockSpec((tm, k), lambda i, j: (i, 0)),
                  pl.BlockSpec((None, k, tn), lambda i, j: (layer, 0, j))] + r_in,
        out_specs=[pl.BlockSpec((tm, tn), lambda i, j: (i, j))] + r_out,
        out_shape=[jax.ShapeDtypeStruct((m, n), out_dtype)] + r_shapes,
        name=name,
        compiler_params=_params(("arbitrary", "arbitrary"), blocks, extra_bytes=_nbytes((tm, tn), F32)),
    )(a, w, *[rd.weight for rd in riders])
    return (out[0], out[1:]) if riders else out[0]


def _matmul_cat_kernel(a0_ref, a1_ref, w0_ref, w1_ref, o_ref):
    acc = jnp.dot(a0_ref[...], w0_ref[...], preferred_element_type=F32)
    acc += jnp.dot(a1_ref[...], w1_ref[...], preferred_element_type=F32)
    o_ref[...] = acc.astype(o_ref.dtype)


def _matmul_cat(a0, a1, w, layer, *, tm, tn, out_dtype):
    m, k0 = a0.shape
    assert a1.shape == (m, k0) and w.shape[1] == 2 * k0
    n = w.shape[2]
    tm, tn = min(tm, m), min(tn, n)
    blocks = [((tm, 2 * k0), a0.dtype), ((2 * k0, tn), w.dtype), ((tm, tn), out_dtype)]
    return pl.pallas_call(
        _matmul_cat_kernel,
        grid=(m // tm, n // tn),
        in_specs=[pl.BlockSpec((tm, k0), lambda i, j: (i, 0)),
                  pl.BlockSpec((tm, k0), lambda i, j: (i, 0)),
                  pl.BlockSpec((None, k0, tn), lambda i, j: (layer, 0, j)),
                  pl.BlockSpec((None, k0, tn), lambda i, j: (layer, 1, j))],
        out_specs=pl.BlockSpec((tm, tn), lambda i, j: (i, j)),
        out_shape=jax.ShapeDtypeStruct((m, n), out_dtype),
        name="out_proj",
        compiler_params=_params(("parallel", "arbitrary"), blocks, extra_bytes=_nbytes((tm, tn), F32)),
    )(a0, a1, w, w)


def _in_proj_kernel(a_ref, w_ref, wdt_ref, cw_ref, cb_ref, o_ref, dt_ref, carry_ref,
                    *, rc, j_act, j_conv, tiles_per_seq):
    i = pl.program_id(0)
    j = pl.program_id(1)
    tm, tn = o_ref.shape

    @pl.when(j == 0)
    def _():
        dt_ref[...] = lax.dot_general(a_ref[...], wdt_ref[...], NT_DIMS, preferred_element_type=F32)

    def row_chunks(epilogue):
        for r in range(tm // rc):
            rows = slice(r * rc, (r + 1) * rc)
            acc = lax.dot_general(a_ref[rows, :], w_ref[...], NT_DIMS, preferred_element_type=F32)
            o_ref[rows, :] = epilogue(acc).astype(o_ref.dtype)

    @pl.when(j < j_act)
    def _():
        o_ref[...] = lax.dot_general(a_ref[...], w_ref[...], NT_DIMS, preferred_element_type=F32).astype(o_ref.dtype)

    @pl.when(jnp.logical_and(j >= j_act, j < j_conv))
    def _():
        row_chunks(_silu)

    @pl.when(j >= j_conv)
    def _():
        jc = j - j_conv
        tail = [jnp.where(i % tiles_per_seq == 0, 0.0, carry_ref[jc])]

        first_rows = lax.broadcasted_iota(jnp.int32, (V7X_SUBLANES, tn), 0)

        def delayed(acc, k):
            rolled = pltpu.roll(acc, k, 0)
            head = jnp.where(first_rows < k, pltpu.roll(tail[0], k, 0), rolled[:V7X_SUBLANES])
            return jnp.concatenate([head, rolled[V7X_SUBLANES:]], axis=0)

        def conv_silu(acc):
            y = cb_ref[...] + cw_ref[SSD_CONV - 1:SSD_CONV, :] * acc
            for k in range(1, SSD_CONV):
                y = y + cw_ref[SSD_CONV - 1 - k:SSD_CONV - k, :] * delayed(acc, k)
            tail[0] = acc[rc - V7X_SUBLANES:, :]
            return _silu(y)

        row_chunks(conv_silu)
        carry_ref[jc] = tail[0]


def _in_proj(h, w, w_dt, layer, conv_w, conv_b, *, seq, n_cols, act_col, conv_col):
    m, k = h.shape
    tm, tn = min(1024, seq), 1024
    assert n_cols % tn == 0 and act_col % tn == 0 and conv_col % tn == 0
    j_act, j_conv, n_j = act_col // tn, conv_col // tn, n_cols // tn
    assert conv_w.shape[1] == n_cols - conv_col
    lanes = w_dt.shape[1]
    blocks = [((tm, k), BF16), ((k, tn), BF16), ((k, lanes), BF16), ((tm, tn), BF16), ((tm, lanes), F32)]
    conv_block = lambda i, j: (0, jnp.maximum(j - j_conv, 0))
    return pl.pallas_call(
        functools.partial(_in_proj_kernel, rc=min(256, tm), j_act=j_act, j_conv=j_conv,
                          tiles_per_seq=seq // tm),
        grid=(m // tm, n_j),
        in_specs=[pl.BlockSpec((tm, k), lambda i, j: (i, 0)),
                  pl.BlockSpec((None, tn, k), lambda i, j: (layer, j, 0)),
                  pl.BlockSpec((None, lanes, k), lambda i, j: (layer, 0, 0)),
                  pl.BlockSpec((SSD_CONV, tn), conv_block),
                  pl.BlockSpec((1, tn), conv_block)],
        out_specs=[pl.BlockSpec((tm, tn), lambda i, j: (i, j)),
                   pl.BlockSpec((tm, lanes), lambda i, j: (i, 0))],
        out_shape=[jax.ShapeDtypeStruct((m, n_cols), BF16), jax.ShapeDtypeStruct((m, lanes), F32)],
        name="in_proj",
        scratch_shapes=[pltpu.VMEM((n_j - j_conv, V7X_SUBLANES, tn), F32)],
        compiler_params=_params(("arbitrary", "arbitrary"), blocks, extra_bytes=3 * _nbytes((tm, tn), F32)),
    )(h, w, w_dt, conv_w, conv_b.reshape(1, -1))


def _attn_kernel(slopes_ref, q_ref, k_ref, v_ref, diag_ref, lam_ref, nw_ref, o_ref, qs_ref, m_ref, l_ref,
                 acc_ref, *, tile, rb, rb_diag, lam_init):
    h = pl.program_id(1)
    i = pl.program_id(2)
    slope = slopes_ref[h] * LOG2_E
    dh = ATT_HEAD_DIM
    nt = (((1,), (1,)), ((), ()))
    for mp in range(2):
        q_mp = q_ref[:, mp * dh:(mp + 1) * dh].astype(F32) * (dh ** -0.5 * LOG2_E)
        qs_ref[mp] = q_mp.astype(BF16)

    lanes = V7X_LANES

    def lane_tiled(x, width):
        return jnp.concatenate([x] * (width // lanes), axis=1)

    def update(rows, mp, s, tile_bias, v_t, first):
        nrows, kw = s.shape
        smax = jnp.broadcast_to(jnp.max(s, axis=-1, keepdims=True), (nrows, lanes)) + tile_bias
        if first:
            m_new = smax
        else:
            m_old = m_ref[mp, rows]
            m_new = jnp.maximum(m_old, smax)
            alpha = jnp.exp2(m_old - m_new)
        p = jnp.exp2(s - lane_tiled(m_new - tile_bias, kw))
        psum = p[:, 0:lanes]
        for c in range(1, kw // lanes):
            psum = psum + p[:, c * lanes:(c + 1) * lanes]
        pv = jnp.dot(p.astype(BF16), v_t, preferred_element_type=F32)
        if first:
            l_ref[mp, rows] = psum
            acc_ref[mp, rows] = pv
        else:
            l_ref[mp, rows] = alpha * l_ref[mp, rows] + psum
            acc_ref[mp, rows] = lane_tiled(alpha, pv.shape[1]) * acc_ref[mp, rows] + pv
        m_ref[mp, rows] = m_new

    k_diag = pl.multiple_of(i * tile, tile)
    diag_bias = slope * (i * tile).astype(F32)
    for r in range(tile // rb_diag):
        rows = slice(r * rb_diag, (r + 1) * rb_diag)
        kw = (r + 1) * rb_diag
        bias = slope * diag_ref[rows, 0:kw]
        v_t = v_ref[pl.ds(k_diag, kw), :]
        for mp in range(2):
            kt = k_ref[pl.ds(k_diag, kw), mp * dh:(mp + 1) * dh]
            s = lax.dot_general(qs_ref[mp, rows], kt, nt, preferred_element_type=F32) + bias
            update(rows, mp, s, diag_bias, v_t, first=True)

    sweep = max(tile, ATT_SWEEP_KEYS)
    tiles_per_sweep = sweep // tile
    assert tiles_per_sweep in (1, 2)
    col_bias = slope * lax.broadcasted_iota(jnp.int32, (1, sweep), 1).astype(F32)

    def visible_keys(k0, kw):
        tile_bias = slope * k0.astype(F32)
        v_t = v_ref[pl.ds(k0, kw), :]
        for r in range(tile // rb):
            rows = slice(r * rb, (r + 1) * rb)
            for mp in range(2):
                kt = k_ref[pl.ds(k0, kw), mp * dh:(mp + 1) * dh]
                s = lax.dot_general(qs_ref[mp, rows], kt, nt, preferred_element_type=F32)
                update(rows, mp, s + col_bias[:, :kw], tile_bias, v_t, first=False)

    def body(jj, carry):
        visible_keys(pl.multiple_of(jj * sweep, sweep), sweep)
        return carry

    lax.fori_loop(0, i // tiles_per_sweep, body, 0)

    if tiles_per_sweep == 2:
        @pl.when(i % 2 == 1)
        def _():
            visible_keys(pl.multiple_of((i - 1) * tile, tile), tile)

    lp = lam_ref[...]
    lam = (jnp.exp(jnp.sum(lp[0:1] * lp[1:2], axis=-1, keepdims=True))
           - jnp.exp(jnp.sum(lp[2:3] * lp[3:4], axis=-1, keepdims=True)) + lam_init)
    inv_l = [1.0 / jnp.sum(l_ref[mp], axis=-1, keepdims=True) for mp in range(2)]
    o = acc_ref[0] * inv_l[0] - lam * (acc_ref[1] * inv_l[1])
    ms = jnp.mean(o * o, axis=-1, keepdims=True)
    y = o * lax.rsqrt(ms + RMS_EPS) * nw_ref[...] * (1.0 - lam_init)
    o_ref[...] = y.astype(o_ref.dtype)


def _attention(proj, lam_params, norm_w, *, seq, n_heads, lam_init, make_riders=None):
    m = proj.shape[0]
    batch = m // seq
    tile = min(ATT_TILE, seq)
    n_q = seq // tile
    dv = ATT_VALUE_DIM
    slopes = jnp.asarray(np.array([2.0 ** (-8.0 * (hh + 1) / n_heads) for hh in range(n_heads)],
                                  dtype=np.float32))
    tq = np.arange(tile)[:, None]
    tk = np.arange(tile)[None, :]
    diag_table = jnp.asarray(np.where(tk // CHUNK <= tq // CHUNK, tq - np.abs(tq - tk), DIAG_MASK_PER_SLOPE)
                             .astype(np.float32))
    riders = make_riders((batch, n_heads, n_q)) if make_riders else ()
    r_in, r_out, r_shapes = _rider_specs(riders, n_trailing_index_args=1)
    blocks = [((tile, dv), BF16), ((seq, dv), BF16), ((seq, dv), BF16), ((tile, tile), F32),
              ((tile, dv), BF16)] + _rider_blocks(riders)
    scratch_bytes = (2 * _nbytes((2, tile, V7X_LANES), F32) + _nbytes((2, tile, dv), F32)
                     + 6 * _nbytes((tile, tile), F32))
    grid_spec = pltpu.PrefetchScalarGridSpec(
        num_scalar_prefetch=1,
        grid=(batch, n_heads, n_q),
        in_specs=[pl.BlockSpec((tile, dv), lambda b, h, i, s: (b * n_q + i, h)),
                  pl.BlockSpec((seq, dv), lambda b, h, i, s: (b, n_heads + h)),
                  pl.BlockSpec((seq, dv), lambda b, h, i, s: (b, 2 * n_heads + h)),
                  pl.BlockSpec((tile, tile), lambda b, h, i, s: (0, 0)),
                  pl.BlockSpec((4, ATT_HEAD_DIM), lambda b, h, i, s: (0, 0)),
                  pl.BlockSpec((1, dv), lambda b, h, i, s: (0, 0))] + r_in,
        out_specs=[pl.BlockSpec((tile, dv), lambda b, h, i, s: (b * n_q + i, h))] + r_out,
        scratch_shapes=[pltpu.VMEM((2, tile, ATT_HEAD_DIM), BF16), pltpu.VMEM((2, tile, V7X_LANES), F32),
                        pltpu.VMEM((2, tile, V7X_LANES), F32), pltpu.VMEM((2, tile, dv), F32)],
    )
    body = functools.partial(_attn_kernel, tile=tile, rb=min(ATT_ROW_BLOCK, tile),
                             rb_diag=min(ATT_DIAG_ROW_BLOCK, tile), lam_init=lam_init)
    out = pl.pallas_call(
        _with_riders(body, 7, 1, len(riders)),
        grid_spec=grid_spec,
        out_shape=[jax.ShapeDtypeStruct((m, n_heads * dv), BF16)] + r_shapes,
        name="diff_attention",
        compiler_params=_params(("arbitrary", "arbitrary", "arbitrary"), blocks, extra_bytes=scratch_bytes),
    )(slopes, proj, proj, proj, diag_table, lam_params, norm_w.reshape(1, dv), *[rd.weight for rd in riders])
    return (out[0], out[1:]) if riders else out[0]


def _ssd_kernel(z_ref, xbc_ref, dt_ref, dtt_ref, dtb_l_ref, alog_l_ref, dtb_s_ref, alog_s_ref, d_ref, nw_ref,
                o_ref, state_ref, *, tile, width):
    t = pl.program_id(1)
    gw = width // SSD_GROUPS
    hpg = gw // SSD_HEAD_DIM
    n = SSD_STATE
    b_off, c_off = width, width + SSD_GROUPS * n

    @pl.when(t == 0)
    def _():
        state_ref[...] = jnp.zeros_like(state_ref)

    row = lax.broadcasted_iota(jnp.int32, (tile, tile), 0)
    col = lax.broadcasted_iota(jnp.int32, (tile, tile), 1)
    causal = row >= col
    dt_s = _softplus(dt_ref[...] + dtb_l_ref[...])
    da_s = dt_s * (-jnp.exp(alog_l_ref[...]))
    acs_s = jnp.dot(causal.astype(F32), da_s, precision=lax.Precision.HIGHEST,
                    preferred_element_type=F32)
    dt_l = _softplus(dtt_ref[...] + dtb_s_ref[...])
    da_l = dt_l * (-jnp.exp(alog_s_ref[...]))
    acs_l = jnp.dot(da_l, (row <= col).astype(F32), precision=lax.Precision.HIGHEST,
                    preferred_element_type=F32)

    head_of_lane = lax.broadcasted_iota(jnp.int32, (1, gw), 1) // SSD_HEAD_DIM

    def per_head(vals):
        out = vals[hpg - 1]
        for r in range(hpg - 2, -1, -1):
            out = jnp.where(head_of_lane == r, vals[r], out)
        return out

    def lane_tiled(x, w):
        return jnp.concatenate([x] * (w // V7X_LANES), axis=1)

    for g in range(SSD_GROUPS):
        xs = xbc_ref[:, g * gw:(g + 1) * gw].astype(F32)
        bm = xbc_ref[:, b_off + g * n:b_off + (g + 1) * n]
        cm = xbc_ref[:, c_off + g * n:c_off + (g + 1) * n]
        heads = [g * hpg + r for r in range(hpg)]
        acs_rep = [jnp.broadcast_to(acs_s[:, hh:hh + 1], (tile, V7X_LANES)) for hh in heads]
        dt_rep = [jnp.broadcast_to(dt_s[:, hh:hh + 1], (tile, V7X_LANES)) for hh in heads]
        acs_sel = per_head([lane_tiled(a, gw) for a in acs_rep])
        last_sel = acs_sel[tile - 1:tile, :]
        xdt = xs * per_head([lane_tiled(d, gw) for d in dt_rep])
        cb = lax.dot_general(cm, bm, (((1,), (1,)), ((), ())), preferred_element_type=F32)
        prev = state_ref[g]
        y = jnp.dot(cm, prev.astype(BF16), preferred_element_type=F32) * jnp.exp(acs_sel)
        for r, hh in enumerate(heads):
            seg = lane_tiled(acs_rep[r], tile) - acs_l[hh:hh + 1, :]
            decay = jnp.exp(jnp.where(causal, seg, MASK_VALUE))
            x_r = jnp.where(head_of_lane == r, xdt, 0.0).astype(BF16)
            y = y + jnp.dot((cb * decay).astype(BF16), x_r, preferred_element_type=F32)
        to_end = jnp.exp(last_sel - acs_sel)
        new_state = lax.dot_general(bm, (xdt * to_end).astype(BF16), (((0,), (0,)), ((), ())),
                                    preferred_element_type=F32)
        state_ref[g] = prev * jnp.exp(last_sel) + new_state
        lanes = slice(g * gw, (g + 1) * gw)
        y = y + d_ref[:, lanes] * xs
        y = y * z_ref[:, lanes].astype(F32)
        y = y * lax.rsqrt(jnp.mean(y * y, axis=-1, keepdims=True) + RMS_EPS) * nw_ref[:, lanes]
        o_ref[:, lanes] = y.astype(o_ref.dtype)


def _ssd(proj, dt_raw, dt_bias, a_log, d_skip, norm_w, *, seq, z_col, xbc_col, make_riders=None):
    m = proj.shape[0]
    batch = m // seq
    n_heads = dt_bias.shape[0]
    width = n_heads * SSD_HEAD_DIM
    conv_dim = width + 2 * SSD_GROUPS * SSD_STATE
    tile = min(256, seq)
    n_t = seq // tile
    lanes = dt_raw.shape[1]

    def lane_vec(v):
        return jnp.zeros((1, lanes), F32).at[0, :n_heads].set(v)

    dtt = dt_raw[:, :n_heads].T
    const = lambda b, t: (0, 0)
    riders = make_riders((batch, n_t)) if make_riders else ()
    r_in, r_out, r_shapes = _rider_specs(riders)
    blocks = [((tile, width), BF16), ((tile, conv_dim), BF16), ((tile, lanes), F32), ((n_heads, tile), F32),
              ((tile, width), BF16)] + _rider_blocks(riders)
    scratch_bytes = _nbytes((SSD_GROUPS, SSD_STATE, width // SSD_GROUPS), F32) + 12 * _nbytes((tile, tile), F32)
    out = pl.pallas_call(
        _with_riders(functools.partial(_ssd_kernel, tile=tile, width=width), 10, 1, len(riders)),
        grid=(batch, n_t),
        in_specs=[pl.BlockSpec((tile, width), lambda b, t: (b * n_t + t, z_col)),
                  pl.BlockSpec((tile, conv_dim), lambda b, t: (b * n_t + t, xbc_col)),
                  pl.BlockSpec((tile, lanes), lambda b, t: (b * n_t + t, 0)),
                  pl.BlockSpec((n_heads, tile), lambda b, t: (0, b * n_t + t)),
                  pl.BlockSpec((1, lanes), const),
                  pl.BlockSpec((1, lanes), const),
                  pl.BlockSpec((n_heads, 1), const),
                  pl.BlockSpec((n_heads, 1), const),
                  pl.BlockSpec((1, width), const),
                  pl.BlockSpec((1, width), const)] + r_in,
        out_specs=[pl.BlockSpec((tile, width), lambda b, t: (b * n_t + t, 0))] + r_out,
        out_shape=[jax.ShapeDtypeStruct((m, width), BF16)] + r_shapes,
        name="ssd_mixer",
        scratch_shapes=[pltpu.VMEM((SSD_GROUPS, SSD_STATE, width // SSD_GROUPS), F32)],
        compiler_params=_params(("arbitrary", "arbitrary"), blocks, extra_bytes=scratch_bytes),
    )(proj, proj, dt_raw, dtt, lane_vec(dt_bias), lane_vec(a_log), dt_bias.reshape(n_heads, 1),
      a_log.reshape(n_heads, 1), jnp.repeat(d_skip, SSD_HEAD_DIM).reshape(1, width), norm_w.reshape(1, width),
      *[rd.weight for rd in riders])
    return (out[0], out[1:]) if riders else out[0]


def _ln_kernel(*refs, alpha, emit_next):
    if emit_next:
        x_ref, br_ref, gate_ref, g_ref, b_ref, scale_ref, shift_ref, xo_ref, h_ref = refs
    else:
        x_ref, br_ref, gate_ref, g_ref, b_ref, xo_ref = refs
    v = alpha * x_ref[...] + gate_ref[...] * br_ref[...].astype(F32)
    mu = jnp.mean(v, axis=-1, keepdims=True)
    dev = v - mu
    var = jnp.mean(dev * dev, axis=-1, keepdims=True)
    y = dev * lax.rsqrt(var + LN_EPS) * g_ref[...] + b_ref[...]
    xo_ref[...] = y
    if emit_next:
        h_ref[...] = (y * (1.0 + scale_ref[...]) + shift_ref[...]).astype(h_ref.dtype)


def _residual_ln(x2, branch, gate, ln_g, ln_b, next_scale, next_shift, *, seq, alpha):
    m, d = x2.shape
    batch = m // seq
    ts = min(256, seq)
    n_s = seq // ts
    emit_next = next_scale is not None
    rows = pl.BlockSpec((ts, d), lambda b, i: (b * n_s + i, 0))
    per_batch = pl.BlockSpec((None, 1, d), lambda b, i: (b, 0, 0))
    shared = pl.BlockSpec((1, d), lambda b, i: (0, 0))
    in_specs = [rows, rows, per_batch, shared, shared]
    args = [x2, branch, gate, ln_g.reshape(1, d), ln_b.reshape(1, d)]
    out_specs = [rows]
    out_shape = [jax.ShapeDtypeStruct((m, d), F32)]
    blocks = [((ts, d), F32), ((ts, d), branch.dtype), ((ts, d), F32)]
    if emit_next:
        in_specs += [per_batch, per_batch]
        args += [next_scale, next_shift]
        out_specs.append(rows)
        out_shape.append(jax.ShapeDtypeStruct((m, d), BF16))
        blocks.append(((ts, d), BF16))
    out = pl.pallas_call(
        functools.partial(_ln_kernel, alpha=alpha, emit_next=emit_next),
        grid=(batch, n_s),
        in_specs=in_specs,
        out_specs=out_specs,
        out_shape=out_shape,
        name="residual_ln",
        compiler_params=_params(("parallel", "parallel"), blocks, extra_bytes=2 * _nbytes((ts, d), F32)),
    )(*args)
    return (out[0], out[1]) if emit_next else (out[0], None)


def _ffn_up_kernel(h_ref, wg_ref, wu_ref, cw_ref, cb_ref, o_ref, carry_ref, *, tm, rc, tiles_per_seq):
    i = pl.program_id(0)
    j = pl.program_id(1)
    tn = o_ref.shape[1]
    tail = jnp.where(i % tiles_per_seq == 0, 0.0, carry_ref[j])
    ridx = lax.broadcasted_iota(jnp.int32, (rc, tn), 0)
    for r in range(tm // rc):
        rows = slice(r * rc, (r + 1) * rc)
        gate = jnp.dot(h_ref[rows, :], wg_ref[...], preferred_element_type=F32)
        up = jnp.dot(h_ref[rows, :], wu_ref[...], preferred_element_type=F32)
        p1 = tail[V7X_SUBLANES - 1:V7X_SUBLANES]
        p2 = tail[V7X_SUBLANES - 2:V7X_SUBLANES - 1]
        g1 = jnp.where(ridx == 0, p1, pltpu.roll(gate, 1, 0))
        g2 = jnp.where(ridx == 0, p2, jnp.where(ridx == 1, p1, pltpu.roll(gate, 2, 0)))
        y = cb_ref[...] + cw_ref[0:1, :] * g2 + cw_ref[1:2, :] * g1 + cw_ref[2:3, :] * gate
        o_ref[rows, :] = (_silu(y) * up).astype(o_ref.dtype)
        tail = gate[rc - V7X_SUBLANES:, :]
    carry_ref[j] = tail


def _ffn_up(h, w_gate, w_up, layer, conv_w, conv_b, *, seq, tn):
    m, d = h.shape
    f = w_gate.shape[2]
    tm = min(2048, seq)
    rc = min(512, tm)
    n_j = f // tn
    blocks = [((tm, d), BF16), ((d, tn), BF16), ((d, tn), BF16), ((tm, tn), BF16)]
    w_spec = pl.BlockSpec((None, d, tn), lambda i, j: (layer, 0, j))
    return pl.pallas_call(
        functools.partial(_ffn_up_kernel, tm=tm, rc=rc, tiles_per_seq=seq // tm),
        grid=(m // tm, n_j),
        in_specs=[pl.BlockSpec((tm, d), lambda i, j: (i, 0)), w_spec, w_spec,
                  pl.BlockSpec((FFN_CONV, tn), lambda i, j: (0, j)),
                  pl.BlockSpec((1, tn), lambda i, j: (0, j))],
        out_specs=pl.BlockSpec((tm, tn), lambda i, j: (i, j)),
        out_shape=jax.ShapeDtypeStruct((m, f), BF16),
        name="ffn_up",
        scratch_shapes=[pltpu.VMEM((n_j, V7X_SUBLANES, tn), F32)],
        compiler_params=_params(("arbitrary", "arbitrary"), blocks,
                                extra_bytes=8 * _nbytes((rc, tn), F32)),
    )(h, w_gate, w_up, conv_w, conv_b.reshape(1, f))


def kernel(x, c, w_mod, b_mod, w_in, diff_lambda, diff_norm_w, ssd_conv_w, ssd_conv_b, ssd_dt_bias,
           ssd_a_log, ssd_d, ssd_norm_w, w_out, ln1_g, ln1_b, w_gate, w_up, ffn_conv_w, ffn_conv_b,
           w_down, ln2_g, ln2_b):
    batch, seq, d = x.shape
    depth = w_mod.shape[0]
    m = batch * seq
    n_ssd_heads = ssd_dt_bias.shape[1]
    ssd_width = n_ssd_heads * SSD_HEAD_DIM
    conv_dim = ssd_conv_w.shape[2]
    n_in = w_in.shape[2]
    att_width = (n_in - n_ssd_heads - conv_dim - ssd_width) // 3
    n_att_heads = att_width // ATT_VALUE_DIM
    n_main = n_in - n_ssd_heads
    assert att_width == ssd_width and conv_dim == 2 * ssd_width
    alpha = (2.0 * depth) ** 0.25
    ffn_tn = 256

    mod = _modulation(c, w_mod, b_mod)[:, :batch]
    mod = mod.reshape(depth, batch, 6, 1, d)
    shift1, scale1, gate1, shift2, scale2, gate2 = (mod[:, :, s] for s in range(6))

    def row_split(weight, layer):
        def make(grid):
            steps = int(np.prod(grid))
            rows, cols = weight.shape[1:]
            assert rows % (steps * V7X_BF16_SUBLANES) == 0

            def index(*ids):
                flat = ids[0]
                for extent, idx in zip(grid[1:], ids[1:]):
                    flat = flat * extent + idx
                return flat, 0
            return _CastRider(weight, layer, (rows // steps, cols), index)
        return make

    def tile_split(weight, layer):
        def make(grid):
            n_b, n_t = grid
            rows, cols = weight.shape[1:]
            assert rows % (n_t * V7X_BF16_SUBLANES) == 0 and cols % (n_b * V7X_LANES) == 0
            return _CastRider(weight, layer, (rows // n_t, cols // n_b), lambda b, t: (t, b))
        return make

    x2 = x.reshape(m, d)
    h = _modulate(x2, scale1[0], shift1[0], seq)
    w_in_t = jnp.swapaxes(w_in, 1, 2).astype(BF16)
    w_dt = jnp.pad(w_in_t[:, n_main:, :], ((0, 0), (0, V7X_LANES - n_ssd_heads), (0, 0)))
    for l in range(depth):
        last = l == depth - 1
        proj, dt_raw = _in_proj(h, w_in_t, w_dt, l, ssd_conv_w[l], ssd_conv_b[l], seq=seq, n_cols=n_main,
                                act_col=3 * att_width, conv_col=3 * att_width + ssd_width)
        lam_init = 0.8 - 0.6 * float(np.exp(-0.3 * l))
        y_att, (w_gate_l, w_up_l) = _attention(
            proj, diff_lambda[l], diff_norm_w[l], seq=seq, n_heads=n_att_heads, lam_init=lam_init,
            make_riders=lambda grid: [row_split(w_gate, l)(grid), row_split(w_up, l)(grid)])
        y_ssd, (w_down_l, w_out_l) = _ssd(
            proj, dt_raw, ssd_dt_bias[l], ssd_a_log[l], ssd_d[l],
            ssd_norm_w[l], seq=seq, z_col=3 * att_width // ssd_width,
            xbc_col=(3 * att_width + ssd_width) // conv_dim,
            make_riders=lambda grid: [tile_split(w_down, l)(grid), tile_split(w_out, l)(grid)])
        mix = _matmul_cat(y_att, y_ssd, w_out_l, 0, tm=1024, tn=1024, out_dtype=BF16)
        x2, h = _residual_ln(x2, mix, gate1[l], ln1_g[l], ln1_b[l], scale2[l], shift2[l],
                             seq=seq, alpha=alpha)
        act = _ffn_up(h, w_gate_l, w_up_l, 0, ffn_conv_w[l], ffn_conv_b[l], seq=seq, tn=ffn_tn)
        ffn = _matmul(act, w_down_l, 0, tm=512, tn=512, out_dtype=BF16, name="ffn_down")
        x2, h = _residual_ln(x2, ffn, gate2[l], ln2_g[l], ln2_b[l],
                             None if last else scale1[l + 1], None if last else shift1[l + 1],
                             seq=seq, alpha=alpha)
    return x2.reshape(batch, seq, d)
```

```python
import functools
from typing import Callable, NamedTuple, Tuple

import numpy as np
import jax
import jax.numpy as jnp
from jax import lax
from jax.experimental import pallas as pl
from jax.experimental.pallas import tpu as pltpu

F32 = jnp.float32
BF16 = jnp.bfloat16

CHUNK = 64
ATT_HEAD_DIM = 128
ATT_VALUE_DIM = 2 * ATT_HEAD_DIM
SSD_HEAD_DIM = 64
SSD_GROUPS = 8
SSD_STATE = 128
SSD_CONV = 4
FFN_CONV = 3
LN_EPS = 1e-5
RMS_EPS = 1e-5

V7X_LANES = 128
V7X_SUBLANES = 8
V7X_BF16_SUBLANES = 16
V7X_VMEM_BYTES = 64 * 1024 * 1024
VMEM_REQUEST_CAP = V7X_VMEM_BYTES - 8 * 1024 * 1024

MASK_VALUE = -1e30
LOG2_E = 1.4426950408889634
NT_DIMS = (((1,), (1,)), ((), ()))
ATT_TILE = 1024
ATT_SWEEP_KEYS = 1024
ATT_ROW_BLOCK = 128
ATT_DIAG_ROW_BLOCK = 256
DIAG_MASK_PER_SLOPE = -1e32


def _nbytes(shape, dtype):
    return int(np.prod(shape)) * jnp.dtype(dtype).itemsize


def _params(semantics, blocks, extra_bytes=0):
    need = 2 * sum(_nbytes(s, d) for s, d in blocks) + extra_bytes + (4 << 20)
    return pltpu.CompilerParams(dimension_semantics=semantics,
                                vmem_limit_bytes=int(min(max(need, 16 << 20), VMEM_REQUEST_CAP)))


def _silu(v):
    u = 0.5 * v
    return u * jnp.tanh(u) + u


def _softplus(v):
    return jnp.maximum(v, 0.0) + jnp.log(1.0 + jnp.exp(-jnp.abs(v)))


class _CastRider(NamedTuple):
    weight: jax.Array
    layer: int
    block: Tuple[int, int]
    index: Callable


def _rider_specs(riders, n_trailing_index_args=0):
    in_specs, out_specs, out_shapes = [], [], []
    for rd in riders:
        def idx(*g, rd=rd, lead=()):
            g = g[:len(g) - n_trailing_index_args] if n_trailing_index_args else g
            return lead + tuple(rd.index(*g))
        in_specs.append(pl.BlockSpec((None,) + rd.block, functools.partial(idx, lead=(rd.layer,))))
        out_specs.append(pl.BlockSpec(rd.block, idx))
        out_shapes.append(jax.ShapeDtypeStruct(rd.weight.shape[1:], BF16))
    return in_specs, out_specs, out_shapes


def _rider_blocks(riders):
    return [(rd.block, F32) for rd in riders] + [(rd.block, BF16) for rd in riders]


def _with_riders(body, n_leading, n_out, n_riders):
    if not n_riders:
        return body

    def wrapped(*refs):
        rider_in = refs[n_leading:n_leading + n_riders]
        host_out = refs[n_leading + n_riders:n_leading + n_riders + n_out]
        rider_out = refs[n_leading + n_riders + n_out:n_leading + 2 * n_riders + n_out]
        for src, dst in zip(rider_in, rider_out):
            dst[...] = src[...].astype(dst.dtype)
        body(*refs[:n_leading], *host_out, *refs[n_leading + 2 * n_riders + n_out:])

    return wrapped


def _mod_kernel(c_ref, w_ref, b_ref, o_ref):
    @pl.when(pl.program_id(2) == 0)
    def _():
        o_ref[...] = jnp.broadcast_to(b_ref[...], o_ref.shape)

    c_act = _silu(c_ref[...]).astype(BF16)
    o_ref[...] += jnp.dot(c_act, w_ref[...].astype(BF16), preferred_element_type=F32)


def _modulation(c, w_mod, b_mod):
    depth, d, n = w_mod.shape
    rows = V7X_SUBLANES
    c_pad = jnp.zeros((rows, d), F32).at[:c.shape[0]].set(c)
    tk, tn = min(1024, d), min(2048, n)
    blocks = [((rows, tk), F32), ((tk, tn), F32), ((1, tn), F32), ((rows, tn), F32)]
    return pl.pallas_call(
        _mod_kernel,
        grid=(depth, n // tn, d // tk),
        in_specs=[pl.BlockSpec((rows, tk), lambda l, j, k: (0, k)),
                  pl.BlockSpec((None, tk, tn), lambda l, j, k: (l, k, j)),
                  pl.BlockSpec((None, 1, tn), lambda l, j, k: (l, 0, j))],
        out_specs=pl.BlockSpec((None, rows, tn), lambda l, j, k: (l, 0, j)),
        out_shape=jax.ShapeDtypeStruct((depth, rows, n), F32),
        name="adaln_mod",
        compiler_params=_params(("parallel", "parallel", "arbitrary"), blocks,
                                extra_bytes=_nbytes((tk, tn), BF16)),
    )(c_pad, w_mod, b_mod.reshape(depth, 1, n))


def _modulate_kernel(x_ref, scale_ref, shift_ref, h_ref):
    h_ref[...] = (x_ref[...] * (1.0 + scale_ref[...]) + shift_ref[...]).astype(h_ref.dtype)


def _modulate(x2, scale, shift, seq):
    m, d = x2.shape
    batch = m // seq
    ts = min(256, seq)
    n_s = seq // ts
    vec = pl.BlockSpec((None, 1, d), lambda b, i: (b, 0, 0))
    blocks = [((ts, d), F32), ((ts, d), BF16)]
    return pl.pallas_call(
        _modulate_kernel,
        grid=(batch, n_s),
        in_specs=[pl.BlockSpec((ts, d), lambda b, i: (b * n_s + i, 0)), vec, vec],
        out_specs=pl.BlockSpec((ts, d), lambda b, i: (b * n_s + i, 0)),
        out_shape=jax.ShapeDtypeStruct((m, d), BF16),
        name="modulate",
        compiler_params=_params(("parallel", "parallel"), blocks),
    )(x2, scale, shift)


def _matmul_kernel(a_ref, w_ref, o_ref):
    o_ref[...] = jnp.dot(a_ref[...], w_ref[...], preferred_element_type=F32).astype(o_ref.dtype)


def _matmul(a, w, *, tm, tn, out_dtype, name):
    m, k = a.shape
    n = w.shape[1]
    tm, tn = min(tm, m), min(tn, n)
    blocks = [((tm, k), a.dtype), ((k, tn), w.dtype), ((tm, tn), out_dtype)]
    return pl.pallas_call(
        _matmul_kernel,
        grid=(m // tm, n // tn),
        in_specs=[pl.BlockSpec((tm, k), lambda i, j: (i, 0)),
                  pl.BlockSpec((k, tn), lambda i, j: (0, j))],
        out_specs=pl.BlockSpec((tm, tn), lambda i, j: (i, j)),
        out_shape=jax.ShapeDtypeStruct((m, n), out_dtype),
        name=name,
        compiler_params=_params(("arbitrary", "arbitrary"), blocks, extra_bytes=_nbytes((tm, tn), F32)),
    )(a, w)


def _matmul_cat_kernel(a0_ref, a1_ref, w0_ref, w1_ref, o_ref):
    acc = jnp.dot(a0_ref[...], w0_ref[...], preferred_element_type=F32)
    acc += jnp.dot(a1_ref[...], w1_ref[...], preferred_element_type=F32)
    o_ref[...] = acc.astype(o_ref.dtype)


def _matmul_cat(a0, a1, w, *, tm, tn, out_dtype):
    m, k0 = a0.shape
    assert a1.shape == (m, k0) and w.shape[0] == 2 * k0
    n = w.shape[1]
    tm, tn = min(tm, m), min(tn, n)
    blocks = [((tm, 2 * k0), a0.dtype), ((2 * k0, tn), w.dtype), ((tm, tn), out_dtype)]
    return pl.pallas_call(
        _matmul_cat_kernel,
        grid=(m // tm, n // tn),
        in_specs=[pl.BlockSpec((tm, k0), lambda i, j: (i, 0)),
                  pl.BlockSpec((tm, k0), lambda i, j: (i, 0)),
                  pl.BlockSpec((k0, tn), lambda i, j: (0, j)),
                  pl.BlockSpec((k0, tn), lambda i, j: (1, j))],
        out_specs=pl.BlockSpec((tm, tn), lambda i, j: (i, j)),
        out_shape=jax.ShapeDtypeStruct((m, n), out_dtype),
        name="out_proj",
        compiler_params=_params(("parallel", "arbitrary"), blocks, extra_bytes=_nbytes((tm, tn), F32)),
    )(a0, a1, w, w)


def _in_proj_kernel(a_ref, w_ref, wdt_ref, o_ref, dt_ref, *, rc, j_act, j_act_end):
    j = pl.program_id(1)
    tm = o_ref.shape[0]

    @pl.when(j == 0)
    def _():
        dt_ref[...] = lax.dot_general(a_ref[...], wdt_ref[...], NT_DIMS, preferred_element_type=F32)

    activated = jnp.logical_and(j >= j_act, j < j_act_end)

    @pl.when(jnp.logical_not(activated))
    def _():
        o_ref[...] = lax.dot_general(a_ref[...], w_ref[...], NT_DIMS, preferred_element_type=F32).astype(o_ref.dtype)

    @pl.when(activated)
    def _():
        for r in range(tm // rc):
            rows = slice(r * rc, (r + 1) * rc)
            acc = lax.dot_general(a_ref[rows, :], w_ref[...], NT_DIMS, preferred_element_type=F32)
            o_ref[rows, :] = _silu(acc).astype(o_ref.dtype)


def _in_proj(h, w, w_dt, layer, *, seq, n_cols, act_col, act_end):
    m, k = h.shape
    tm, tn = min(1024, seq), 1024
    assert n_cols % tn == 0 and act_col % tn == 0 and act_end % tn == 0
    lanes = w_dt.shape[1]
    blocks = [((tm, k), BF16), ((k, tn), BF16), ((k, lanes), BF16), ((tm, tn), BF16), ((tm, lanes), F32)]
    return pl.pallas_call(
        functools.partial(_in_proj_kernel, rc=min(256, tm), j_act=act_col // tn, j_act_end=act_end // tn),
        grid=(m // tm, n_cols // tn),
        in_specs=[pl.BlockSpec((tm, k), lambda i, j: (i, 0)),
                  pl.BlockSpec((None, tn, k), lambda i, j: (layer, j, 0)),
                  pl.BlockSpec((None, lanes, k), lambda i, j: (layer, 0, 0))],
        out_specs=[pl.BlockSpec((tm, tn), lambda i, j: (i, j)),
                   pl.BlockSpec((tm, lanes), lambda i, j: (i, 0))],
        out_shape=[jax.ShapeDtypeStruct((m, n_cols), BF16), jax.ShapeDtypeStruct((m, lanes), F32)],
        name="in_proj",
        compiler_params=_params(("arbitrary", "arbitrary"), blocks, extra_bytes=3 * _nbytes((tm, tn), F32)),
    )(h, w, w_dt)


def _attn_kernel(slopes_ref, q_ref, k_ref, v_ref, diag_ref, lam_ref, nw_ref, o_ref, qs_ref, m_ref, l_ref,
                 acc_ref, *, tile, rb, rb_diag, lam_init):
    h = pl.program_id(1)
    i = pl.program_id(2)
    slope = slopes_ref[h] * LOG2_E
    dh = ATT_HEAD_DIM
    nt = (((1,), (1,)), ((), ()))
    for mp in range(2):
        q_mp = q_ref[:, mp * dh:(mp + 1) * dh].astype(F32) * (dh ** -0.5 * LOG2_E)
        qs_ref[mp] = q_mp.astype(BF16)

    lanes = V7X_LANES

    def lane_tiled(x, width):
        return jnp.concatenate([x] * (width // lanes), axis=1)

    def update(rows, mp, s, tile_bias, v_t, first):
        nrows, kw = s.shape
        smax = jnp.broadcast_to(jnp.max(s, axis=-1, keepdims=True), (nrows, lanes)) + tile_bias
        if first:
            m_new = smax
        else:
            m_old = m_ref[mp, rows]
            m_new = jnp.maximum(m_old, smax)
            alpha = jnp.exp2(m_old - m_new)
        p = jnp.exp2(s - lane_tiled(m_new - tile_bias, kw))
        psum = p[:, 0:lanes]
        for c in range(1, kw // lanes):
            psum = psum + p[:, c * lanes:(c + 1) * lanes]
        pv = jnp.dot(p.astype(BF16), v_t, preferred_element_type=F32)
        if first:
            l_ref[mp, rows] = psum
            acc_ref[mp, rows] = pv
        else:
            l_ref[mp, rows] = alpha * l_ref[mp, rows] + psum
            acc_ref[mp, rows] = lane_tiled(alpha, pv.shape[1]) * acc_ref[mp, rows] + pv
        m_ref[mp, rows] = m_new

    k_diag = pl.multiple_of(i * tile, tile)
    diag_bias = slope * (i * tile).astype(F32)
    for r in range(tile // rb_diag):
        rows = slice(r * rb_diag, (r + 1) * rb_diag)
        kw = (r + 1) * rb_diag
        bias = slope * diag_ref[rows, 0:kw]
        v_t = v_ref[pl.ds(k_diag, kw), :]
        for mp in range(2):
            kt = k_ref[pl.ds(k_diag, kw), mp * dh:(mp + 1) * dh]
            s = lax.dot_general(qs_ref[mp, rows], kt, nt, preferred_element_type=F32) + bias
            update(rows, mp, s, diag_bias, v_t, first=True)

    sweep = max(tile, ATT_SWEEP_KEYS)
    tiles_per_sweep = sweep // tile
    assert tiles_per_sweep in (1, 2)
    col_bias = slope * lax.broadcasted_iota(jnp.int32, (1, sweep), 1).astype(F32)

    def visible_keys(k0, kw):
        tile_bias = slope * k0.astype(F32)
        v_t = v_ref[pl.ds(k0, kw), :]
        for r in range(tile // rb):
            rows = slice(r * rb, (r + 1) * rb)
            for mp in range(2):
                kt = k_ref[pl.ds(k0, kw), mp * dh:(mp + 1) * dh]
                s = lax.dot_general(qs_ref[mp, rows], kt, nt, preferred_element_type=F32)
                update(rows, mp, s + col_bias[:, :kw], tile_bias, v_t, first=False)

    def body(jj, carry):
        visible_keys(pl.multiple_of(jj * sweep, sweep), sweep)
        return carry

    lax.fori_loop(0, i // tiles_per_sweep, body, 0)

    if tiles_per_sweep == 2:
        @pl.when(i % 2 == 1)
        def _():
            visible_keys(pl.multiple_of((i - 1) * tile, tile), tile)

    lp = lam_ref[...]
    lam = (jnp.exp(jnp.sum(lp[0:1] * lp[1:2], axis=-1, keepdims=True))
           - jnp.exp(jnp.sum(lp[2:3] * lp[3:4], axis=-1, keepdims=True)) + lam_init)
    inv_l = [1.0 / jnp.sum(l_ref[mp], axis=-1, keepdims=True) for mp in range(2)]
    o = acc_ref[0] * inv_l[0] - lam * (acc_ref[1] * inv_l[1])
    ms = jnp.mean(o * o, axis=-1, keepdims=True)
    y = o * lax.rsqrt(ms + RMS_EPS) * nw_ref[...] * (1.0 - lam_init)
    o_ref[...] = y.astype(o_ref.dtype)


def _attention(proj, lam_params, norm_w, *, seq, n_heads, lam_init, make_riders=None):
    m = proj.shape[0]
    batch = m // seq
    tile = min(ATT_TILE, seq)
    n_q = seq // tile
    dv = ATT_VALUE_DIM
    slopes = jnp.asarray(np.array([2.0 ** (-8.0 * (hh + 1) / n_heads) for hh in range(n_heads)],
                                  dtype=np.float32))
    tq = np.arange(tile)[:, None]
    tk = np.arange(tile)[None, :]
    diag_table = jnp.asarray(np.where(tk // CHUNK <= tq // CHUNK, tq - np.abs(tq - tk), DIAG_MASK_PER_SLOPE)
                             .astype(np.float32))
    riders = make_riders((batch, n_heads, n_q)) if make_riders else ()
    r_in, r_out, r_shapes = _rider_specs(riders, n_trailing_index_args=1)
    blocks = [((tile, dv), BF16), ((seq, dv), BF16), ((seq, dv), BF16), ((tile, tile), F32),
              ((tile, dv), BF16)] + _rider_blocks(riders)
    scratch_bytes = (2 * _nbytes((2, tile, V7X_LANES), F32) + _nbytes((2, tile, dv), F32)
                     + 6 * _nbytes((tile, tile), F32))
    grid_spec = pltpu.PrefetchScalarGridSpec(
        num_scalar_prefetch=1,
        grid=(batch, n_heads, n_q),
        in_specs=[pl.BlockSpec((tile, dv), lambda b, h, i, s: (b * n_q + i, h)),
                  pl.BlockSpec((seq, dv), lambda b, h, i, s: (b, n_heads + h)),
                  pl.BlockSpec((seq, dv), lambda b, h, i, s: (b, 2 * n_heads + h)),
                  pl.BlockSpec((tile, tile), lambda b, h, i, s: (0, 0)),
                  pl.BlockSpec((4, ATT_HEAD_DIM), lambda b, h, i, s: (0, 0)),
                  pl.BlockSpec((1, dv), lambda b, h, i, s: (0, 0))] + r_in,
        out_specs=[pl.BlockSpec((tile, dv), lambda b, h, i, s: (b * n_q + i, h))] + r_out,
        scratch_shapes=[pltpu.VMEM((2, tile, ATT_HEAD_DIM), BF16), pltpu.VMEM((2, tile, V7X_LANES), F32),
                        pltpu.VMEM((2, tile, V7X_LANES), F32), pltpu.VMEM((2, tile, dv), F32)],
    )
    body = functools.partial(_attn_kernel, tile=tile, rb=min(ATT_ROW_BLOCK, tile),
                             rb_diag=min(ATT_DIAG_ROW_BLOCK, tile), lam_init=lam_init)
    out = pl.pallas_call(
        _with_riders(body, 7, 1, len(riders)),
        grid_spec=grid_spec,
        out_shape=[jax.ShapeDtypeStruct((m, n_heads * dv), BF16)] + r_shapes,
        name="diff_attention",
        compiler_params=_params(("arbitrary", "arbitrary", "arbitrary"), blocks, extra_bytes=scratch_bytes),
    )(slopes, proj, proj, proj, diag_table, lam_params, norm_w.reshape(1, dv), *[rd.weight for rd in riders])
    return (out[0], out[1:]) if riders else out[0]


def _ssd_kernel(z_ref, xbc_ref, halo_ref, dt_ref, dtt_ref, cw_ref, cb_ref, dtb_l_ref, alog_l_ref,
                dtb_s_ref, alog_s_ref, d_ref, nw_ref, o_ref, state_ref, xd_ref, *, tile, width):
    t = pl.program_id(1)
    gw = width // SSD_GROUPS
    hpg = gw // SSD_HEAD_DIM
    n = SSD_STATE
    b_off, c_off = width, width + SSD_GROUPS * n

    @pl.when(t == 0)
    def _():
        state_ref[...] = jnp.zeros_like(state_ref)

    row = lax.broadcasted_iota(jnp.int32, (tile, tile), 0)
    col = lax.broadcasted_iota(jnp.int32, (tile, tile), 1)
    causal = row >= col
    dt_s = _softplus(dt_ref[...] + dtb_l_ref[...])
    da_s = dt_s * (-jnp.exp(alog_l_ref[...]))
    acs_s = jnp.dot(causal.astype(F32), da_s, precision=lax.Precision.HIGHEST,
                    preferred_element_type=F32)
    dt_l = _softplus(dtt_ref[...] + dtb_s_ref[...])
    da_l = dt_l * (-jnp.exp(alog_s_ref[...]))
    acs_l = jnp.dot(da_l, (row <= col).astype(F32), precision=lax.Precision.HIGHEST,
                    preferred_element_type=F32)

    head_of_lane = lax.broadcasted_iota(jnp.int32, (1, gw), 1) // SSD_HEAD_DIM

    def per_head(vals):
        out = vals[hpg - 1]
        for r in range(hpg - 2, -1, -1):
            out = jnp.where(head_of_lane == r, vals[r], out)
        return out

    def lane_tiled(x, w):
        return jnp.concatenate([x] * (w // V7X_LANES), axis=1)

    shifts = jnp.concatenate([(row - col == k).astype(BF16) for k in range(1, SSD_CONV)], axis=0)
    xd_ref[...] = jnp.dot(shifts, xbc_ref[...], preferred_element_type=F32)
    sub = V7X_SUBLANES
    before = jnp.where(t > 0, halo_ref[halo_ref.shape[0] - sub:, :].astype(F32), 0.0)
    head = jnp.concatenate([before, xbc_ref[0:sub, :].astype(F32)], axis=0)
    for k in range(1, SSD_CONV):
        xd_ref[(k - 1) * tile:(k - 1) * tile + sub, :] = head[sub - k:2 * sub - k]

    def conv_silu(c0, w):
        lanes = slice(c0, c0 + w)
        y = cb_ref[:, lanes] + cw_ref[SSD_CONV - 1:SSD_CONV, lanes] * xbc_ref[:, lanes].astype(F32)
        for k in range(1, SSD_CONV):
            y = y + cw_ref[SSD_CONV - 1 - k:SSD_CONV - k, lanes] * xd_ref[(k - 1) * tile:k * tile, lanes]
        return _silu(y)

    for g in range(SSD_GROUPS):
        xs = conv_silu(g * gw, gw)
        bm = conv_silu(b_off + g * n, n).astype(BF16)
        cm = conv_silu(c_off + g * n, n).astype(BF16)
        heads = [g * hpg + r for r in range(hpg)]
        acs_rep = [jnp.broadcast_to(acs_s[:, hh:hh + 1], (tile, V7X_LANES)) for hh in heads]
        dt_rep = [jnp.broadcast_to(dt_s[:, hh:hh + 1], (tile, V7X_LANES)) for hh in heads]
        acs_sel = per_head([lane_tiled(a, gw) for a in acs_rep])
        last_sel = acs_sel[tile - 1:tile, :]
        xdt = xs * per_head([lane_tiled(d, gw) for d in dt_rep])
        cb = lax.dot_general(cm, bm, (((1,), (1,)), ((), ())), preferred_element_type=F32)
        prev = state_ref[g]
        y = jnp.dot(cm, prev.astype(BF16), preferred_element_type=F32) * jnp.exp(acs_sel)
        for r, hh in enumerate(heads):
            seg = lane_tiled(acs_rep[r], tile) - acs_l[hh:hh + 1, :]
            decay = jnp.exp(jnp.where(causal, seg, MASK_VALUE))
            x_r = jnp.where(head_of_lane == r, xdt, 0.0).astype(BF16)
            y = y + jnp.dot((cb * decay).astype(BF16), x_r, preferred_element_type=F32)
        to_end = jnp.exp(last_sel - acs_sel)
        new_state = lax.dot_general(bm, (xdt * to_end).astype(BF16), (((0,), (0,)), ((), ())),
                                    preferred_element_type=F32)
        state_ref[g] = prev * jnp.exp(last_sel) + new_state
        lanes = slice(g * gw, (g + 1) * gw)
        y = y + d_ref[:, lanes] * xs
        y = y * z_ref[:, lanes].astype(F32)
        y = y * lax.rsqrt(jnp.mean(y * y, axis=-1, keepdims=True) + RMS_EPS) * nw_ref[:, lanes]
        o_ref[:, lanes] = y.astype(o_ref.dtype)


def _ssd(proj, dt_raw, conv_w, conv_b, dt_bias, a_log, d_skip, norm_w, *, seq, z_col, xbc_col, make_riders=None):
    m = proj.shape[0]
    batch = m // seq
    n_heads = dt_bias.shape[0]
    width = n_heads * SSD_HEAD_DIM
    conv_dim = conv_w.shape[1]
    tile = min(256, seq)
    n_t = seq // tile
    halo = V7X_BF16_SUBLANES
    halo_per_tile = tile // halo
    lanes = dt_raw.shape[1]

    def lane_vec(v):
        return jnp.zeros((1, lanes), F32).at[0, :n_heads].set(v)

    dtt = dt_raw[:, :n_heads].T
    const = lambda b, t: (0, 0)
    riders = make_riders((batch, n_t)) if make_riders else ()
    r_in, r_out, r_shapes = _rider_specs(riders)
    blocks = [((tile, width), BF16), ((tile, conv_dim), BF16), ((halo, conv_dim), BF16),
              ((tile, lanes), F32), ((n_heads, tile), F32), ((tile, width), BF16)] + _rider_blocks(riders)
    scratch_bytes = _nbytes((SSD_GROUPS, SSD_STATE, width // SSD_GROUPS), F32) + 12 * _nbytes((tile, tile), F32) \
        + _nbytes(((SSD_CONV - 1) * tile, conv_dim), F32)
    out = pl.pallas_call(
        _with_riders(functools.partial(_ssd_kernel, tile=tile, width=width), 13, 1, len(riders)),
        grid=(batch, n_t),
        in_specs=[pl.BlockSpec((tile, width), lambda b, t: (b * n_t + t, z_col)),
                  pl.BlockSpec((tile, conv_dim), lambda b, t: (b * n_t + t, xbc_col)),
                  pl.BlockSpec((halo, conv_dim),
                               lambda b, t: (jnp.maximum((b * n_t + t) * halo_per_tile - 1, 0), xbc_col)),
                  pl.BlockSpec((tile, lanes), lambda b, t: (b * n_t + t, 0)),
                  pl.BlockSpec((n_heads, tile), lambda b, t: (0, b * n_t + t)),
                  pl.BlockSpec((SSD_CONV, conv_dim), const),
                  pl.BlockSpec((1, conv_dim), const),
                  pl.BlockSpec((1, lanes), const),
                  pl.BlockSpec((1, lanes), const),
                  pl.BlockSpec((n_heads, 1), const),
                  pl.BlockSpec((n_heads, 1), const),
                  pl.BlockSpec((1, width), const),
                  pl.BlockSpec((1, width), const)] + r_in,
        out_specs=[pl.BlockSpec((tile, width), lambda b, t: (b * n_t + t, 0))] + r_out,
        out_shape=[jax.ShapeDtypeStruct((m, width), BF16)] + r_shapes,
        name="ssd_mixer",
        scratch_shapes=[pltpu.VMEM((SSD_GROUPS, SSD_STATE, width // SSD_GROUPS), F32),
                        pltpu.VMEM(((SSD_CONV - 1) * tile, conv_dim), F32)],
        compiler_params=_params(("arbitrary", "arbitrary"), blocks, extra_bytes=scratch_bytes),
    )(proj, proj, proj, dt_raw, dtt, conv_w, conv_b.reshape(1, conv_dim), lane_vec(dt_bias), lane_vec(a_log),
      dt_bias.reshape(n_heads, 1), a_log.reshape(n_heads, 1),
      jnp.repeat(d_skip, SSD_HEAD_DIM).reshape(1, width), norm_w.reshape(1, width),
      *[rd.weight for rd in riders])
    return (out[0], out[1:]) if riders else out[0]


def _ln_kernel(*refs, alpha, emit_next):
    if emit_next:
        x_ref, br_ref, gate_ref, g_ref, b_ref, scale_ref, shift_ref, xo_ref, h_ref = refs
    else:
        x_ref, br_ref, gate_ref, g_ref, b_ref, xo_ref = refs
    v = alpha * x_ref[...] + gate_ref[...] * br_ref[...].astype(F32)
    mu = jnp.mean(v, axis=-1, keepdims=True)
    dev = v - mu
    var = jnp.mean(dev * dev, axis=-1, keepdims=True)
    y = dev * lax.rsqrt(var + LN_EPS) * g_ref[...] + b_ref[...]
    xo_ref[...] = y
    if emit_next:
        h_ref[...] = (y * (1.0 + scale_ref[...]) + shift_ref[...]).astype(h_ref.dtype)


def _residual_ln(x2, branch, gate, ln_g, ln_b, next_scale, next_shift, *, seq, alpha):
    m, d = x2.shape
    batch = m // seq
    ts = min(256, seq)
    n_s = seq // ts
    emit_next = next_scale is not None
    rows = pl.BlockSpec((ts, d), lambda b, i: (b * n_s + i, 0))
    per_batch = pl.BlockSpec((None, 1, d), lambda b, i: (b, 0, 0))
    shared = pl.BlockSpec((1, d), lambda b, i: (0, 0))
    in_specs = [rows, rows, per_batch, shared, shared]
    args = [x2, branch, gate, ln_g.reshape(1, d), ln_b.reshape(1, d)]
    out_specs = [rows]
    out_shape = [jax.ShapeDtypeStruct((m, d), F32)]
    blocks = [((ts, d), F32), ((ts, d), branch.dtype), ((ts, d), F32)]
    if emit_next:
        in_specs += [per_batch, per_batch]
        args += [next_scale, next_shift]
        out_specs.append(rows)
        out_shape.append(jax.ShapeDtypeStruct((m, d), BF16))
        blocks.append(((ts, d), BF16))
    out = pl.pallas_call(
        functools.partial(_ln_kernel, alpha=alpha, emit_next=emit_next),
        grid=(batch, n_s),
        in_specs=in_specs,
        out_specs=out_specs,
        out_shape=out_shape,
        name="residual_ln",
        compiler_params=_params(("parallel", "parallel"), blocks, extra_bytes=2 * _nbytes((ts, d), F32)),
    )(*args)
    return (out[0], out[1]) if emit_next else (out[0], None)


def _ffn_up_kernel(h_ref, wg_ref, wu_ref, cw_ref, cb_ref, o_ref, carry_ref, *, tm, rc, tiles_per_seq):
    i = pl.program_id(0)
    j = pl.program_id(1)
    tn = o_ref.shape[1]
    tail = jnp.where(i % tiles_per_seq == 0, 0.0, carry_ref[j])
    ridx = lax.broadcasted_iota(jnp.int32, (rc, tn), 0)
    for r in range(tm // rc):
        rows = slice(r * rc, (r + 1) * rc)
        gate = jnp.dot(h_ref[rows, :], wg_ref[...], preferred_element_type=F32)
        up = jnp.dot(h_ref[rows, :], wu_ref[...], preferred_element_type=F32)
        p1 = tail[V7X_SUBLANES - 1:V7X_SUBLANES]
        p2 = tail[V7X_SUBLANES - 2:V7X_SUBLANES - 1]
        g1 = jnp.where(ridx == 0, p1, pltpu.roll(gate, 1, 0))
        g2 = jnp.where(ridx == 0, p2, jnp.where(ridx == 1, p1, pltpu.roll(gate, 2, 0)))
        y = cb_ref[...] + cw_ref[0:1, :] * g2 + cw_ref[1:2, :] * g1 + cw_ref[2:3, :] * gate
        o_ref[rows, :] = (_silu(y) * up).astype(o_ref.dtype)
        tail = gate[rc - V7X_SUBLANES:, :]
    carry_ref[j] = tail


def _ffn_up(h, w_gate, w_up, conv_w, conv_b, *, seq, tn):
    m, d = h.shape
    f = w_gate.shape[1]
    tm = min(2048, seq)
    rc = min(512, tm)
    n_j = f // tn
    blocks = [((tm, d), BF16), ((d, tn), BF16), ((d, tn), BF16), ((tm, tn), BF16)]
    w_spec = pl.BlockSpec((d, tn), lambda i, j: (0, j))
    return pl.pallas_call(
        functools.partial(_ffn_up_kernel, tm=tm, rc=rc, tiles_per_seq=seq // tm),
        grid=(m // tm, n_j),
        in_specs=[pl.BlockSpec((tm, d), lambda i, j: (i, 0)), w_spec, w_spec,
                  pl.BlockSpec((FFN_CONV, tn), lambda i, j: (0, j)),
                  pl.BlockSpec((1, tn), lambda i, j: (0, j))],
        out_specs=pl.BlockSpec((tm, tn), lambda i, j: (i, j)),
        out_shape=jax.ShapeDtypeStruct((m, f), BF16),
        name="ffn_up",
        scratch_shapes=[pltpu.VMEM((n_j, V7X_SUBLANES, tn), F32)],
        compiler_params=_params(("arbitrary", "arbitrary"), blocks,
                                extra_bytes=8 * _nbytes((rc, tn), F32)),
    )(h, w_gate, w_up, conv_w, conv_b.reshape(1, f))


def kernel(x, c, w_mod, b_mod, w_in, diff_lambda, diff_norm_w, ssd_conv_w, ssd_conv_b, ssd_dt_bias,
           ssd_a_log, ssd_d, ssd_norm_w, w_out, ln1_g, ln1_b, w_gate, w_up, ffn_conv_w, ffn_conv_b,
           w_down, ln2_g, ln2_b):
    batch, seq, d = x.shape
    depth = w_mod.shape[0]
    m = batch * seq
    n_ssd_heads = ssd_dt_bias.shape[1]
    ssd_width = n_ssd_heads * SSD_HEAD_DIM
    conv_dim = ssd_conv_w.shape[2]
    n_in = w_in.shape[2]
    att_width = (n_in - n_ssd_heads - conv_dim - ssd_width) // 3
    n_att_heads = att_width // ATT_VALUE_DIM
    n_main = n_in - n_ssd_heads
    assert att_width == ssd_width and conv_dim == 2 * ssd_width
    alpha = (2.0 * depth) ** 0.25
    ffn_tn = 256

    mod = _modulation(c, w_mod, b_mod)[:, :batch]
    mod = mod.reshape(depth, batch, 6, 1, d)
    shift1, scale1, gate1, shift2, scale2, gate2 = (mod[:, :, s] for s in range(6))

    def row_split(weight, layer):
        def make(grid):
            steps = int(np.prod(grid))
            rows, cols = weight.shape[1:]
            assert rows % (steps * V7X_BF16_SUBLANES) == 0

            def index(*ids):
                flat = ids[0]
                for extent, idx in zip(grid[1:], ids[1:]):
                    flat = flat * extent + idx
                return flat, 0
            return _CastRider(weight, layer, (rows // steps, cols), index)
        return make

    def tile_split(weight, layer):
        def make(grid):
            n_b, n_t = grid
            rows, cols = weight.shape[1:]
            assert rows % (n_t * V7X_BF16_SUBLANES) == 0 and cols % (n_b * V7X_LANES) == 0
            return _CastRider(weight, layer, (rows // n_t, cols // n_b), lambda b, t: (t, b))
        return make

    x2 = x.reshape(m, d)
    h = _modulate(x2, scale1[0], shift1[0], seq)
    w_in_t = jnp.swapaxes(w_in, 1, 2).astype(BF16)
    w_dt = jnp.pad(w_in_t[:, n_main:, :], ((0, 0), (0, V7X_LANES - n_ssd_heads), (0, 0)))
    for l in range(depth):
        last = l == depth - 1
        proj, dt_raw = _in_proj(h, w_in_t, w_dt, l, seq=seq, n_cols=n_main,
                                act_col=3 * att_width, act_end=3 * att_width + ssd_width)
        lam_init = 0.8 - 0.6 * float(np.exp(-0.3 * l))
        y_att, (w_gate_l, w_up_l) = _attention(
            proj, diff_lambda[l], diff_norm_w[l], seq=seq, n_heads=n_att_heads, lam_init=lam_init,
            make_riders=lambda grid: [row_split(w_gate, l)(grid), row_split(w_up, l)(grid)])
        y_ssd, (w_down_l, w_out_l) = _ssd(
            proj, dt_raw, ssd_conv_w[l], ssd_conv_b[l], ssd_dt_bias[l], ssd_a_log[l], ssd_d[l],
            ssd_norm_w[l], seq=seq, z_col=3 * att_width // ssd_width,
            xbc_col=(3 * att_width + ssd_width) // conv_dim,
            make_riders=lambda grid: [tile_split(w_down, l)(grid), tile_split(w_out, l)(grid)])
        mix = _matmul_cat(y_att, y_ssd, w_out_l, tm=1024, tn=1024, out_dtype=BF16)
        x2, h = _residual_ln(x2, mix, gate1[l], ln1_g[l], ln1_b[l], scale2[l], shift2[l],
                             seq=seq, alpha=alpha)
        act = _ffn_up(h, w_gate_l, w_up_l, ffn_conv_w[l], ffn_conv_b[l], seq=seq, tn=ffn_tn)
        ffn = _matmul(act, w_down_l, tm=512, tn=512, out_dtype=BF16, name="ffn_down")
        x2, h = _residual_ln(x2, ffn, gate2[l], ln2_g[l], ln2_b[l],
                             None if last else scale1[l + 1], None if last else shift1[l + 1],
                             seq=seq, alpha=alpha)
    return x2.reshape(batch, seq, d)
```

```python
import functools
from typing import Callable, NamedTuple, Tuple

import numpy as np
import jax
import jax.numpy as jnp
from jax import lax
from jax.experimental import pallas as pl
from jax.experimental.pallas import tpu as pltpu

F32 = jnp.float32
BF16 = jnp.bfloat16

CHUNK = 64
ATT_HEAD_DIM = 128
ATT_VALUE_DIM = 2 * ATT_HEAD_DIM
SSD_HEAD_DIM = 64
SSD_GROUPS = 8
SSD_STATE = 128
SSD_CONV = 4
FFN_CONV = 3
LN_EPS = 1e-5
RMS_EPS = 1e-5

V7X_LANES = 128
V7X_SUBLANES = 8
V7X_BF16_SUBLANES = 16
V7X_VMEM_BYTES = 64 * 1024 * 1024
VMEM_REQUEST_CAP = V7X_VMEM_BYTES - 8 * 1024 * 1024

MASK_VALUE = -1e30
LOG2_E = 1.4426950408889634
NT_DIMS = (((1,), (1,)), ((), ()))
ATT_TILE = 1024
ATT_SWEEP_KEYS = 1024
ATT_ROW_BLOCK = 128
ATT_DIAG_ROW_BLOCK = 256
DIAG_MASK_PER_SLOPE = -1e32


def _nbytes(shape, dtype):
    return int(np.prod(shape)) * jnp.dtype(dtype).itemsize


def _params(semantics, blocks, extra_bytes=0):
    need = 2 * sum(_nbytes(s, d) for s, d in blocks) + extra_bytes + (4 << 20)
    return pltpu.CompilerParams(dimension_semantics=semantics,
                                vmem_limit_bytes=int(min(max(need, 16 << 20), VMEM_REQUEST_CAP)))


def _silu(v):
    u = 0.5 * v
    return u * jnp.tanh(u) + u


def _softplus(v):
    return jnp.maximum(v, 0.0) + jnp.log(1.0 + jnp.exp(-jnp.abs(v)))


class _CastRider(NamedTuple):
    weight: jax.Array
    layer: int
    block: Tuple[int, int]
    index: Callable


def _rider_specs(riders, n_trailing_index_args=0):
    in_specs, out_specs, out_shapes = [], [], []
    for rd in riders:
        def idx(*g, rd=rd, lead=()):
            g = g[:len(g) - n_trailing_index_args] if n_trailing_index_args else g
            return lead + tuple(rd.index(*g))
        in_specs.append(pl.BlockSpec((None,) + rd.block, functools.partial(idx, lead=(rd.layer,))))
        out_specs.append(pl.BlockSpec(rd.block, idx))
        out_shapes.append(jax.ShapeDtypeStruct(rd.weight.shape[1:], BF16))
    return in_specs, out_specs, out_shapes


def _rider_blocks(riders):
    return [(rd.block, F32) for rd in riders] + [(rd.block, BF16) for rd in riders]


def _with_riders(body, n_leading, n_out, n_riders):
    if not n_riders:
        return body

    def wrapped(*refs):
        rider_in = refs[n_leading:n_leading + n_riders]
        host_out = refs[n_leading + n_riders:n_leading + n_riders + n_out]
        rider_out = refs[n_leading + n_riders + n_out:n_leading + 2 * n_riders + n_out]
        for src, dst in zip(rider_in, rider_out):
            dst[...] = src[...].astype(dst.dtype)
        body(*refs[:n_leading], *host_out, *refs[n_leading + 2 * n_riders + n_out:])

    return wrapped


def _mod_kernel(c_ref, w_ref, b_ref, o_ref):
    @pl.when(pl.program_id(2) == 0)
    def _():
        o_ref[...] = jnp.broadcast_to(b_ref[...], o_ref.shape)

    c_act = _silu(c_ref[...]).astype(BF16)
    o_ref[...] += jnp.dot(c_act, w_ref[...].astype(BF16), preferred_element_type=F32)


def _modulation(c, w_mod, b_mod):
    depth, d, n = w_mod.shape
    rows = V7X_SUBLANES
    c_pad = jnp.zeros((rows, d), F32).at[:c.shape[0]].set(c)
    tk, tn = min(1024, d), min(2048, n)
    blocks = [((rows, tk), F32), ((tk, tn), F32), ((1, tn), F32), ((rows, tn), F32)]
    return pl.pallas_call(
        _mod_kernel,
        grid=(depth, n // tn, d // tk),
        in_specs=[pl.BlockSpec((rows, tk), lambda l, j, k: (0, k)),
                  pl.BlockSpec((None, tk, tn), lambda l, j, k: (l, k, j)),
                  pl.BlockSpec((None, 1, tn), lambda l, j, k: (l, 0, j))],
        out_specs=pl.BlockSpec((None, rows, tn), lambda l, j, k: (l, 0, j)),
        out_shape=jax.ShapeDtypeStruct((depth, rows, n), F32),
        name="adaln_mod",
        compiler_params=_params(("parallel", "parallel", "arbitrary"), blocks,
                                extra_bytes=_nbytes((tk, tn), BF16)),
    )(c_pad, w_mod, b_mod.reshape(depth, 1, n))


def _modulate_kernel(x_ref, scale_ref, shift_ref, h_ref):
    h_ref[...] = (x_ref[...] * (1.0 + scale_ref[...]) + shift_ref[...]).astype(h_ref.dtype)


def _modulate(x2, scale, shift, seq):
    m, d = x2.shape
    batch = m // seq
    ts = min(256, seq)
    n_s = seq // ts
    vec = pl.BlockSpec((None, 1, d), lambda b, i: (b, 0, 0))
    blocks = [((ts, d), F32), ((ts, d), BF16)]
    return pl.pallas_call(
        _modulate_kernel,
        grid=(batch, n_s),
        in_specs=[pl.BlockSpec((ts, d), lambda b, i: (b * n_s + i, 0)), vec, vec],
        out_specs=pl.BlockSpec((ts, d), lambda b, i: (b * n_s + i, 0)),
        out_shape=jax.ShapeDtypeStruct((m, d), BF16),
        name="modulate",
        compiler_params=_params(("parallel", "parallel"), blocks),
    )(x2, scale, shift)


def _matmul_ksplit_kernel(a_ref, w_ref, o_ref, acc_ref, *, n_k):
    k = pl.program_id(1)
    j = pl.program_id(2)
    part = jnp.dot(a_ref[...], w_ref[...], preferred_element_type=F32)

    @pl.when(k == 0)
    def _():
        acc_ref[j] = part

    @pl.when(jnp.logical_and(k > 0, k < n_k - 1))
    def _():
        acc_ref[j] += part

    @pl.when(k == n_k - 1)
    def _():
        o_ref[...] = (acc_ref[j] + part).astype(o_ref.dtype)


def _matmul_ksplit(a, w, *, tm, tn, n_k, out_dtype, name):
    m, k = a.shape
    n = w.shape[1]
    assert n_k >= 2 and k % n_k == 0
    tk = k // n_k
    blocks = [((tm, tk), a.dtype), ((tk, tn), w.dtype), ((tm, tn), out_dtype)]
    return pl.pallas_call(
        functools.partial(_matmul_ksplit_kernel, n_k=n_k),
        grid=(m // tm, n_k, n // tn),
        in_specs=[pl.BlockSpec((tm, tk), lambda i, kk, j: (i, kk)),
                  pl.BlockSpec((tk, tn), lambda i, kk, j: (kk, j))],
        out_specs=pl.BlockSpec((tm, tn), lambda i, kk, j: (i, jnp.where(kk == n_k - 1, j, 0))),
        out_shape=jax.ShapeDtypeStruct((m, n), out_dtype),
        name=name,
        scratch_shapes=[pltpu.VMEM((n // tn, tm, tn), F32)],
        compiler_params=_params(("arbitrary", "arbitrary", "arbitrary"), blocks,
                                extra_bytes=_nbytes((n // tn + 1, tm, tn), F32)),
    )(a, w)


def _matmul_cat_kernel(a0_ref, a1_ref, w0_ref, w1_ref, o_ref):
    acc = jnp.dot(a0_ref[...], w0_ref[...], preferred_element_type=F32)
    acc += jnp.dot(a1_ref[...], w1_ref[...], preferred_element_type=F32)
    o_ref[...] = acc.astype(o_ref.dtype)


def _matmul_cat(a0, a1, w, *, tm, tn, out_dtype):
    m, k0 = a0.shape
    assert a1.shape == (m, k0) and w.shape[0] == 2 * k0
    n = w.shape[1]
    tm, tn = min(tm, m), min(tn, n)
    blocks = [((tm, 2 * k0), a0.dtype), ((2 * k0, tn), w.dtype), ((tm, tn), out_dtype)]
    return pl.pallas_call(
        _matmul_cat_kernel,
        grid=(m // tm, n // tn),
        in_specs=[pl.BlockSpec((tm, k0), lambda i, j: (i, 0)),
                  pl.BlockSpec((tm, k0), lambda i, j: (i, 0)),
                  pl.BlockSpec((k0, tn), lambda i, j: (0, j)),
                  pl.BlockSpec((k0, tn), lambda i, j: (1, j))],
        out_specs=pl.BlockSpec((tm, tn), lambda i, j: (i, j)),
        out_shape=jax.ShapeDtypeStruct((m, n), out_dtype),
        name="out_proj",
        compiler_params=_params(("parallel", "arbitrary"), blocks, extra_bytes=_nbytes((tm, tn), F32)),
    )(a0, a1, w, w)


def _in_proj_kernel(a_ref, w_ref, wdt_ref, o_ref, dt_ref, *, rc, j_act, j_act_end):
    j = pl.program_id(1)
    tm = o_ref.shape[0]

    @pl.when(j == 0)
    def _():
        dt_ref[...] = lax.dot_general(a_ref[...], wdt_ref[...], NT_DIMS, preferred_element_type=F32)

    activated = jnp.logical_and(j >= j_act, j < j_act_end)

    @pl.when(jnp.logical_not(activated))
    def _():
        o_ref[...] = lax.dot_general(a_ref[...], w_ref[...], NT_DIMS, preferred_element_type=F32).astype(o_ref.dtype)

    @pl.when(activated)
    def _():
        for r in range(tm // rc):
            rows = slice(r * rc, (r + 1) * rc)
            acc = lax.dot_general(a_ref[rows, :], w_ref[...], NT_DIMS, preferred_element_type=F32)
            o_ref[rows, :] = _silu(acc).astype(o_ref.dtype)


def _in_proj(h, w, w_dt, layer, *, seq, n_cols, act_col, act_end):
    m, k = h.shape
    tm, tn = min(1024, seq), 1024
    assert n_cols % tn == 0 and act_col % tn == 0 and act_end % tn == 0
    lanes = w_dt.shape[1]
    blocks = [((tm, k), BF16), ((k, tn), BF16), ((k, lanes), BF16), ((tm, tn), BF16), ((tm, lanes), F32)]
    return pl.pallas_call(
        functools.partial(_in_proj_kernel, rc=min(256, tm), j_act=act_col // tn, j_act_end=act_end // tn),
        grid=(m // tm, n_cols // tn),
        in_specs=[pl.BlockSpec((tm, k), lambda i, j: (i, 0)),
                  pl.BlockSpec((None, tn, k), lambda i, j: (layer, j, 0)),
                  pl.BlockSpec((None, lanes, k), lambda i, j: (layer, 0, 0))],
        out_specs=[pl.BlockSpec((tm, tn), lambda i, j: (i, j)),
                   pl.BlockSpec((tm, lanes), lambda i, j: (i, 0))],
        out_shape=[jax.ShapeDtypeStruct((m, n_cols), BF16), jax.ShapeDtypeStruct((m, lanes), F32)],
        name="in_proj",
        compiler_params=_params(("arbitrary", "arbitrary"), blocks, extra_bytes=3 * _nbytes((tm, tn), F32)),
    )(h, w, w_dt)


def _attn_kernel(slopes_ref, q_ref, k_ref, v_ref, diag_ref, lam_ref, nw_ref, o_ref, qs_ref, m_ref, l_ref,
                 acc_ref, *, tile, rb, rb_diag, lam_init):
    h = pl.program_id(1)
    i = pl.program_id(2)
    slope = slopes_ref[h] * LOG2_E
    dh = ATT_HEAD_DIM
    nt = (((1,), (1,)), ((), ()))
    for mp in range(2):
        q_mp = q_ref[:, mp * dh:(mp + 1) * dh].astype(F32) * (dh ** -0.5 * LOG2_E)
        qs_ref[mp] = q_mp.astype(BF16)

    lanes = V7X_LANES

    def lane_tiled(x, width):
        return jnp.concatenate([x] * (width // lanes), axis=1)

    def update(rows, mp, s, tile_bias, v_t, first):
        nrows, kw = s.shape
        smax = jnp.broadcast_to(jnp.max(s, axis=-1, keepdims=True), (nrows, lanes)) + tile_bias
        if first:
            m_new = smax
        else:
            m_old = m_ref[mp, rows]
            m_new = jnp.maximum(m_old, smax)
            alpha = jnp.exp2(m_old - m_new)
        p = jnp.exp2(s - lane_tiled(m_new - tile_bias, kw))
        psum = p[:, 0:lanes]
        for c in range(1, kw // lanes):
            psum = psum + p[:, c * lanes:(c + 1) * lanes]
        pv = jnp.dot(p.astype(BF16), v_t, preferred_element_type=F32)
        if first:
            l_ref[mp, rows] = psum
            acc_ref[mp, rows] = pv
        else:
            l_ref[mp, rows] = alpha * l_ref[mp, rows] + psum
            acc_ref[mp, rows] = lane_tiled(alpha, pv.shape[1]) * acc_ref[mp, rows] + pv
        m_ref[mp, rows] = m_new

    k_diag = pl.multiple_of(i * tile, tile)
    diag_bias = slope * (i * tile).astype(F32)
    for r in range(tile // rb_diag):
        rows = slice(r * rb_diag, (r + 1) * rb_diag)
        kw = (r + 1) * rb_diag
        bias = slope * diag_ref[rows, 0:kw]
        v_t = v_ref[pl.ds(k_diag, kw), :]
        for mp in range(2):
            kt = k_ref[pl.ds(k_diag, kw), mp * dh:(mp + 1) * dh]
            s = lax.dot_general(qs_ref[mp, rows], kt, nt, preferred_element_type=F32) + bias
            update(rows, mp, s, diag_bias, v_t, first=True)

    sweep = max(tile, ATT_SWEEP_KEYS)
    tiles_per_sweep = sweep // tile
    assert tiles_per_sweep in (1, 2)
    col_bias = slope * lax.broadcasted_iota(jnp.int32, (1, sweep), 1).astype(F32)

    def visible_keys(k0, kw):
        tile_bias = slope * k0.astype(F32)
        v_t = v_ref[pl.ds(k0, kw), :]
        for r in range(tile // rb):
            rows = slice(r * rb, (r + 1) * rb)
            for mp in range(2):
                kt = k_ref[pl.ds(k0, kw), mp * dh:(mp + 1) * dh]
                s = lax.dot_general(qs_ref[mp, rows], kt, nt, preferred_element_type=F32)
                update(rows, mp, s + col_bias[:, :kw], tile_bias, v_t, first=False)

    def body(jj, carry):
        visible_keys(pl.multiple_of(jj * sweep, sweep), sweep)
        return carry

    lax.fori_loop(0, i // tiles_per_sweep, body, 0)

    if tiles_per_sweep == 2:
        @pl.when(i % 2 == 1)
        def _():
            visible_keys(pl.multiple_of((i - 1) * tile, tile), tile)

    lp = lam_ref[...]
    lam = (jnp.exp(jnp.sum(lp[0:1] * lp[1:2], axis=-1, keepdims=True))
           - jnp.exp(jnp.sum(lp[2:3] * lp[3:4], axis=-1, keepdims=True)) + lam_init)
    inv_l = [1.0 / jnp.sum(l_ref[mp], axis=-1, keepdims=True) for mp in range(2)]
    o = acc_ref[0] * inv_l[0] - lam * (acc_ref[1] * inv_l[1])
    ms = jnp.mean(o * o, axis=-1, keepdims=True)
    y = o * lax.rsqrt(ms + RMS_EPS) * nw_ref[...] * (1.0 - lam_init)
    o_ref[...] = y.astype(o_ref.dtype)


def _attention(proj, lam_params, norm_w, *, seq, n_heads, lam_init, make_riders=None):
    m = proj.shape[0]
    batch = m // seq
    tile = min(ATT_TILE, seq)
    n_q = seq // tile
    dv = ATT_VALUE_DIM
    slopes = jnp.asarray(np.array([2.0 ** (-8.0 * (hh + 1) / n_heads) for hh in range(n_heads)],
                                  dtype=np.float32))
    tq = np.arange(tile)[:, None]
    tk = np.arange(tile)[None, :]
    diag_table = jnp.asarray(np.where(tk // CHUNK <= tq // CHUNK, tq - np.abs(tq - tk), DIAG_MASK_PER_SLOPE)
                             .astype(np.float32))
    riders = make_riders((batch, n_heads, n_q)) if make_riders else ()
    r_in, r_out, r_shapes = _rider_specs(riders, n_trailing_index_args=1)
    blocks = [((tile, dv), BF16), ((seq, dv), BF16), ((seq, dv), BF16), ((tile, tile), F32),
              ((tile, dv), BF16)] + _rider_blocks(riders)
    scratch_bytes = (2 * _nbytes((2, tile, V7X_LANES), F32) + _nbytes((2, tile, dv), F32)
                     + 6 * _nbytes((tile, tile), F32))
    grid_spec = pltpu.PrefetchScalarGridSpec(
        num_scalar_prefetch=1,
        grid=(batch, n_heads, n_q),
        in_specs=[pl.BlockSpec((tile, dv), lambda b, h, i, s: (b * n_q + i, h)),
                  pl.BlockSpec((seq, dv), lambda b, h, i, s: (b, n_heads + h)),
                  pl.BlockSpec((seq, dv), lambda b, h, i, s: (b, 2 * n_heads + h)),
                  pl.BlockSpec((tile, tile), lambda b, h, i, s: (0, 0)),
                  pl.BlockSpec((4, ATT_HEAD_DIM), lambda b, h, i, s: (0, 0)),
                  pl.BlockSpec((1, dv), lambda b, h, i, s: (0, 0))] + r_in,
        out_specs=[pl.BlockSpec((tile, dv), lambda b, h, i, s: (b * n_q + i, h))] + r_out,
        scratch_shapes=[pltpu.VMEM((2, tile, ATT_HEAD_DIM), BF16), pltpu.VMEM((2, tile, V7X_LANES), F32),
                        pltpu.VMEM((2, tile, V7X_LANES), F32), pltpu.VMEM((2, tile, dv), F32)],
    )
    body = functools.partial(_attn_kernel, tile=tile, rb=min(ATT_ROW_BLOCK, tile),
                             rb_diag=min(ATT_DIAG_ROW_BLOCK, tile), lam_init=lam_init)
    out = pl.pallas_call(
        _with_riders(body, 7, 1, len(riders)),
        grid_spec=grid_spec,
        out_shape=[jax.ShapeDtypeStruct((m, n_heads * dv), BF16)] + r_shapes,
        name="diff_attention",
        compiler_params=_params(("arbitrary", "arbitrary", "arbitrary"), blocks, extra_bytes=scratch_bytes),
    )(slopes, proj, proj, proj, diag_table, lam_params, norm_w.reshape(1, dv), *[rd.weight for rd in riders])
    return (out[0], out[1:]) if riders else out[0]


def _ssd_kernel(z_ref, xbc_ref, halo_ref, dt_ref, dtt_ref, cw_ref, cb_ref, dtb_l_ref, alog_l_ref,
                dtb_s_ref, alog_s_ref, d_ref, nw_ref, o_ref, state_ref, xd_ref, *, tile, width):
    t = pl.program_id(1)
    gw = width // SSD_GROUPS
    hpg = gw // SSD_HEAD_DIM
    n = SSD_STATE
    b_off, c_off = width, width + SSD_GROUPS * n

    @pl.when(t == 0)
    def _():
        state_ref[...] = jnp.zeros_like(state_ref)

    row = lax.broadcasted_iota(jnp.int32, (tile, tile), 0)
    col = lax.broadcasted_iota(jnp.int32, (tile, tile), 1)
    causal = row >= col
    dt_s = _softplus(dt_ref[...] + dtb_l_ref[...])
    da_s = dt_s * (-jnp.exp(alog_l_ref[...]))
    acs_s = jnp.dot(causal.astype(F32), da_s, precision=lax.Precision.HIGHEST,
                    preferred_element_type=F32)
    dt_l = _softplus(dtt_ref[...] + dtb_s_ref[...])
    da_l = dt_l * (-jnp.exp(alog_s_ref[...]))
    acs_l = jnp.dot(da_l, (row <= col).astype(F32), precision=lax.Precision.HIGHEST,
                    preferred_element_type=F32)

    head_of_lane = lax.broadcasted_iota(jnp.int32, (1, gw), 1) // SSD_HEAD_DIM

    def per_head(vals):
        out = vals[hpg - 1]
        for r in range(hpg - 2, -1, -1):
            out = jnp.where(head_of_lane == r, vals[r], out)
        return out

    def lane_tiled(x, w):
        return jnp.concatenate([x] * (w // V7X_LANES), axis=1)

    shifts = jnp.concatenate([(row - col == k).astype(BF16) for k in range(1, SSD_CONV)], axis=0)
    xd_ref[...] = jnp.dot(shifts, xbc_ref[...], preferred_element_type=F32)
    sub = V7X_SUBLANES
    before = jnp.where(t > 0, halo_ref[halo_ref.shape[0] - sub:, :].astype(F32), 0.0)
    head = jnp.concatenate([before, xbc_ref[0:sub, :].astype(F32)], axis=0)
    for k in range(1, SSD_CONV):
        xd_ref[(k - 1) * tile:(k - 1) * tile + sub, :] = head[sub - k:2 * sub - k]

    def conv_silu(c0, w):
        lanes = slice(c0, c0 + w)
        y = cb_ref[:, lanes] + cw_ref[SSD_CONV - 1:SSD_CONV, lanes] * xbc_ref[:, lanes].astype(F32)
        for k in range(1, SSD_CONV):
            y = y + cw_ref[SSD_CONV - 1 - k:SSD_CONV - k, lanes] * xd_ref[(k - 1) * tile:k * tile, lanes]
        return _silu(y)

    for g in range(SSD_GROUPS):
        xs = conv_silu(g * gw, gw)
        bm = conv_silu(b_off + g * n, n).astype(BF16)
        cm = conv_silu(c_off + g * n, n).astype(BF16)
        heads = [g * hpg + r for r in range(hpg)]
        acs_rep = [jnp.broadcast_to(acs_s[:, hh:hh + 1], (tile, V7X_LANES)) for hh in heads]
        dt_rep = [jnp.broadcast_to(dt_s[:, hh:hh + 1], (tile, V7X_LANES)) for hh in heads]
        acs_sel = per_head([lane_tiled(a, gw) for a in acs_rep])
        last_sel = acs_sel[tile - 1:tile, :]
        xdt = xs * per_head([lane_tiled(d, gw) for d in dt_rep])
        cb = lax.dot_general(cm, bm, (((1,), (1,)), ((), ())), preferred_element_type=F32)
        prev = state_ref[g]
        y = jnp.dot(cm, prev.astype(BF16), preferred_element_type=F32) * jnp.exp(acs_sel)
        for r, hh in enumerate(heads):
            seg = lane_tiled(acs_rep[r], tile) - acs_l[hh:hh + 1, :]
            decay = jnp.exp(jnp.where(causal, seg, MASK_VALUE))
            x_r = jnp.where(head_of_lane == r, xdt, 0.0).astype(BF16)
            y = y + jnp.dot((cb * decay).astype(BF16), x_r, preferred_element_type=F32)
        to_end = jnp.exp(last_sel - acs_sel)
        new_state = lax.dot_general(bm, (xdt * to_end).astype(BF16), (((0,), (0,)), ((), ())),
                                    preferred_element_type=F32)
        state_ref[g] = prev * jnp.exp(last_sel) + new_state
        lanes = slice(g * gw, (g + 1) * gw)
        y = y + d_ref[:, lanes] * xs
        y = y * z_ref[:, lanes].astype(F32)
        y = y * lax.rsqrt(jnp.mean(y * y, axis=-1, keepdims=True) + RMS_EPS) * nw_ref[:, lanes]
        o_ref[:, lanes] = y.astype(o_ref.dtype)


def _ssd(proj, dt_raw, conv_w, conv_b, dt_bias, a_log, d_skip, norm_w, *, seq, z_col, xbc_col, make_riders=None):
    m = proj.shape[0]
    batch = m // seq
    n_heads = dt_bias.shape[0]
    width = n_heads * SSD_HEAD_DIM
    conv_dim = conv_w.shape[1]
    tile = min(256, seq)
    n_t = seq // tile
    halo = V7X_BF16_SUBLANES
    halo_per_tile = tile // halo
    lanes = dt_raw.shape[1]

    def lane_vec(v):
        return jnp.zeros((1, lanes), F32).at[0, :n_heads].set(v)

    dtt = dt_raw[:, :n_heads].T
    const = lambda b, t: (0, 0)
    riders = make_riders((batch, n_t)) if make_riders else ()
    r_in, r_out, r_shapes = _rider_specs(riders)
    blocks = [((tile, width), BF16), ((tile, conv_dim), BF16), ((halo, conv_dim), BF16),
              ((tile, lanes), F32), ((n_heads, tile), F32), ((tile, width), BF16)] + _rider_blocks(riders)
    scratch_bytes = _nbytes((SSD_GROUPS, SSD_STATE, width // SSD_GROUPS), F32) + 12 * _nbytes((tile, tile), F32) \
        + _nbytes(((SSD_CONV - 1) * tile, conv_dim), F32)
    out = pl.pallas_call(
        _with_riders(functools.partial(_ssd_kernel, tile=tile, width=width), 13, 1, len(riders)),
        grid=(batch, n_t),
        in_specs=[pl.BlockSpec((tile, width), lambda b, t: (b * n_t + t, z_col)),
                  pl.BlockSpec((tile, conv_dim), lambda b, t: (b * n_t + t, xbc_col)),
                  pl.BlockSpec((halo, conv_dim),
                               lambda b, t: (jnp.maximum((b * n_t + t) * halo_per_tile - 1, 0), xbc_col)),
                  pl.BlockSpec((tile, lanes), lambda b, t: (b * n_t + t, 0)),
                  pl.BlockSpec((n_heads, tile), lambda b, t: (0, b * n_t + t)),
                  pl.BlockSpec((SSD_CONV, conv_dim), const),
                  pl.BlockSpec((1, conv_dim), const),
                  pl.BlockSpec((1, lanes), const),
                  pl.BlockSpec((1, lanes), const),
                  pl.BlockSpec((n_heads, 1), const),
                  pl.BlockSpec((n_heads, 1), const),
                  pl.BlockSpec((1, width), const),
                  pl.BlockSpec((1, width), const)] + r_in,
        out_specs=[pl.BlockSpec((tile, width), lambda b, t: (b * n_t + t, 0))] + r_out,
        out_shape=[jax.ShapeDtypeStruct((m, width), BF16)] + r_shapes,
        name="ssd_mixer",
        scratch_shapes=[pltpu.VMEM((SSD_GROUPS, SSD_STATE, width // SSD_GROUPS), F32),
                        pltpu.VMEM(((SSD_CONV - 1) * tile, conv_dim), F32)],
        compiler_params=_params(("arbitrary", "arbitrary"), blocks, extra_bytes=scratch_bytes),
    )(proj, proj, proj, dt_raw, dtt, conv_w, conv_b.reshape(1, conv_dim), lane_vec(dt_bias), lane_vec(a_log),
      dt_bias.reshape(n_heads, 1), a_log.reshape(n_heads, 1),
      jnp.repeat(d_skip, SSD_HEAD_DIM).reshape(1, width), norm_w.reshape(1, width),
      *[rd.weight for rd in riders])
    return (out[0], out[1:]) if riders else out[0]


def _ln_kernel(*refs, alpha, emit_next):
    if emit_next:
        x_ref, br_ref, gate_ref, g_ref, b_ref, scale_ref, shift_ref, xo_ref, h_ref = refs
    else:
        x_ref, br_ref, gate_ref, g_ref, b_ref, xo_ref = refs
    v = alpha * x_ref[...] + gate_ref[...] * br_ref[...].astype(F32)
    mu = jnp.mean(v, axis=-1, keepdims=True)
    dev = v - mu
    var = jnp.mean(dev * dev, axis=-1, keepdims=True)
    y = dev * lax.rsqrt(var + LN_EPS) * g_ref[...] + b_ref[...]
    xo_ref[...] = y
    if emit_next:
        h_ref[...] = (y * (1.0 + scale_ref[...]) + shift_ref[...]).astype(h_ref.dtype)


def _residual_ln(x2, branch, gate, ln_g, ln_b, next_scale, next_shift, *, seq, alpha):
    m, d = x2.shape
    batch = m // seq
    ts = min(256, seq)
    n_s = seq // ts
    emit_next = next_scale is not None
    rows = pl.BlockSpec((ts, d), lambda b, i: (b * n_s + i, 0))
    per_batch = pl.BlockSpec((None, 1, d), lambda b, i: (b, 0, 0))
    shared = pl.BlockSpec((1, d), lambda b, i: (0, 0))
    in_specs = [rows, rows, per_batch, shared, shared]
    args = [x2, branch, gate, ln_g.reshape(1, d), ln_b.reshape(1, d)]
    out_specs = [rows]
    out_shape = [jax.ShapeDtypeStruct((m, d), F32)]
    blocks = [((ts, d), F32), ((ts, d), branch.dtype), ((ts, d), F32)]
    if emit_next:
        in_specs += [per_batch, per_batch]
        args += [next_scale, next_shift]
        out_specs.append(rows)
        out_shape.append(jax.ShapeDtypeStruct((m, d), BF16))
        blocks.append(((ts, d), BF16))
    out = pl.pallas_call(
        functools.partial(_ln_kernel, alpha=alpha, emit_next=emit_next),
        grid=(batch, n_s),
        in_specs=in_specs,
        out_specs=out_specs,
        out_shape=out_shape,
        name="residual_ln",
        compiler_params=_params(("parallel", "parallel"), blocks, extra_bytes=2 * _nbytes((ts, d), F32)),
    )(*args)
    return (out[0], out[1]) if emit_next else (out[0], None)


def _ffn_up_kernel(h_ref, wg_ref, wu_ref, cw_ref, cb_ref, o_ref, carry_ref, *, tm, rc, tiles_per_seq):
    i = pl.program_id(0)
    j = pl.program_id(1)
    tn = o_ref.shape[1]
    tail = jnp.where(i % tiles_per_seq == 0, 0.0, carry_ref[j])
    ridx = lax.broadcasted_iota(jnp.int32, (rc, tn), 0)
    for r in range(tm // rc):
        rows = slice(r * rc, (r + 1) * rc)
        gate = jnp.dot(h_ref[rows, :], wg_ref[...], preferred_element_type=F32)
        up = jnp.dot(h_ref[rows, :], wu_ref[...], preferred_element_type=F32)
        p1 = tail[V7X_SUBLANES - 1:V7X_SUBLANES]
        p2 = tail[V7X_SUBLANES - 2:V7X_SUBLANES - 1]
        g1 = jnp.where(ridx == 0, p1, pltpu.roll(gate, 1, 0))
        g2 = jnp.where(ridx == 0, p2, jnp.where(ridx == 1, p1, pltpu.roll(gate, 2, 0)))
        y = cb_ref[...] + cw_ref[0:1, :] * g2 + cw_ref[1:2, :] * g1 + cw_ref[2:3, :] * gate
        o_ref[rows, :] = (_silu(y) * up).astype(o_ref.dtype)
        tail = gate[rc - V7X_SUBLANES:, :]
    carry_ref[j] = tail


def _ffn_up(h, w_gate, w_up, conv_w, conv_b, *, seq, tn):
    m, d = h.shape
    f = w_gate.shape[1]
    tm = min(2048, seq)
    rc = min(512, tm)
    n_j = f // tn
    blocks = [((tm, d), BF16), ((d, tn), BF16), ((d, tn), BF16), ((tm, tn), BF16)]
    w_spec = pl.BlockSpec((d, tn), lambda i, j: (0, j))
    return pl.pallas_call(
        functools.partial(_ffn_up_kernel, tm=tm, rc=rc, tiles_per_seq=seq // tm),
        grid=(m // tm, n_j),
        in_specs=[pl.BlockSpec((tm, d), lambda i, j: (i, 0)), w_spec, w_spec,
                  pl.BlockSpec((FFN_CONV, tn), lambda i, j: (0, j)),
                  pl.BlockSpec((1, tn), lambda i, j: (0, j))],
        out_specs=pl.BlockSpec((tm, tn), lambda i, j: (i, j)),
        out_shape=jax.ShapeDtypeStruct((m, f), BF16),
        name="ffn_up",
        scratch_shapes=[pltpu.VMEM((n_j, V7X_SUBLANES, tn), F32)],
        compiler_params=_params(("arbitrary", "arbitrary"), blocks,
                                extra_bytes=8 * _nbytes((rc, tn), F32)),
    )(h, w_gate, w_up, conv_w, conv_b.reshape(1, f))


def kernel(x, c, w_mod, b_mod, w_in, diff_lambda, diff_norm_w, ssd_conv_w, ssd_conv_b, ssd_dt_bias,
           ssd_a_log, ssd_d, ssd_norm_w, w_out, ln1_g, ln1_b, w_gate, w_up, ffn_conv_w, ffn_conv_b,
           w_down, ln2_g, ln2_b):
    batch, seq, d = x.shape
    depth = w_mod.shape[0]
    m = batch * seq
    n_ssd_heads = ssd_dt_bias.shape[1]
    ssd_width = n_ssd_heads * SSD_HEAD_DIM
    conv_dim = ssd_conv_w.shape[2]
    n_in = w_in.shape[2]
    att_width = (n_in - n_ssd_heads - conv_dim - ssd_width) // 3
    n_att_heads = att_width // ATT_VALUE_DIM
    n_main = n_in - n_ssd_heads
    assert att_width == ssd_width and conv_dim == 2 * ssd_width
    alpha = (2.0 * depth) ** 0.25
    ffn_tn = 256

    mod = _modulation(c, w_mod, b_mod)[:, :batch]
    mod = mod.reshape(depth, batch, 6, 1, d)
    shift1, scale1, gate1, shift2, scale2, gate2 = (mod[:, :, s] for s in range(6))

    def row_split(weight, layer):
        def make(grid):
            steps = int(np.prod(grid))
            rows, cols = weight.shape[1:]
            assert rows % (steps * V7X_BF16_SUBLANES) == 0

            def index(*ids):
                flat = ids[0]
                for extent, idx in zip(grid[1:], ids[1:]):
                    flat = flat * extent + idx
                return flat, 0
            return _CastRider(weight, layer, (rows // steps, cols), index)
        return make

    def tile_split(weight, layer):
        def make(grid):
            n_b, n_t = grid
            rows, cols = weight.shape[1:]
            assert rows % (n_t * V7X_BF16_SUBLANES) == 0 and cols % (n_b * V7X_LANES) == 0
            return _CastRider(weight, layer, (rows // n_t, cols // n_b), lambda b, t: (t, b))
        return make

    x2 = x.reshape(m, d)
    h = _modulate(x2, scale1[0], shift1[0], seq)
    w_in_t = jnp.swapaxes(w_in, 1, 2).astype(BF16)
    w_dt = jnp.pad(w_in_t[:, n_main:, :], ((0, 0), (0, V7X_LANES - n_ssd_heads), (0, 0)))
    for l in range(depth):
        last = l == depth - 1
        proj, dt_raw = _in_proj(h, w_in_t, w_dt, l, seq=seq, n_cols=n_main,
                                act_col=3 * att_width, act_end=3 * att_width + ssd_width)
        lam_init = 0.8 - 0.6 * float(np.exp(-0.3 * l))
        y_att, (w_gate_l, w_up_l) = _attention(
            proj, diff_lambda[l], diff_norm_w[l], seq=seq, n_heads=n_att_heads, lam_init=lam_init,
            make_riders=lambda grid: [row_split(w_gate, l)(grid), row_split(w_up, l)(grid)])
        y_ssd, (w_down_l, w_out_l) = _ssd(
            proj, dt_raw, ssd_conv_w[l], ssd_conv_b[l], ssd_dt_bias[l], ssd_a_log[l], ssd_d[l],
            ssd_norm_w[l], seq=seq, z_col=3 * att_width // ssd_width,
            xbc_col=(3 * att_width + ssd_width) // conv_dim,
            make_riders=lambda grid: [tile_split(w_down, l)(grid), tile_split(w_out, l)(grid)])
        mix = _matmul_cat(y_att, y_ssd, w_out_l, tm=1024, tn=1024, out_dtype=BF16)
        x2, h = _residual_ln(x2, mix, gate1[l], ln1_g[l], ln1_b[l], scale2[l], shift2[l],
                             seq=seq, alpha=alpha)
        act = _ffn_up(h, w_gate_l, w_up_l, ffn_conv_w[l], ffn_conv_b[l], seq=seq, tn=ffn_tn)
        ffn = _matmul_ksplit(act, w_down_l, tm=1024, tn=512, n_k=2, out_dtype=BF16, name="ffn_down")
        x2, h = _residual_ln(x2, ffn, gate2[l], ln2_g[l], ln2_b[l],
                             None if last else scale1[l + 1], None if last else shift1[l + 1],
                             seq=seq, alpha=alpha)
    return x2.reshape(batch, seq, d)
```

```python
import functools
from typing import Callable, NamedTuple, Tuple

import numpy as np
import jax
import jax.numpy as jnp
from jax import lax
from jax.experimental import pallas as pl
from jax.experimental.pallas import tpu as pltpu

F32 = jnp.float32
BF16 = jnp.bfloat16

CHUNK = 64
ATT_HEAD_DIM = 128
ATT_VALUE_DIM = 2 * ATT_HEAD_DIM
SSD_HEAD_DIM = 64
SSD_GROUPS = 8
SSD_STATE = 128
SSD_CONV = 4
FFN_CONV = 3
LN_EPS = 1e-5
RMS_EPS = 1e-5

V7X_LANES = 128
V7X_SUBLANES = 8
V7X_BF16_SUBLANES = 16
V7X_VMEM_BYTES = 64 * 1024 * 1024
VMEM_REQUEST_CAP = V7X_VMEM_BYTES - 8 * 1024 * 1024

MASK_VALUE = -1e30
LOG2_E = 1.4426950408889634
NT_DIMS = (((1,), (1,)), ((), ()))
ATT_TILE = 1024
ATT_SWEEP_KEYS = 1024
ATT_ROW_BLOCK = 128
ATT_DIAG_ROW_BLOCK = 256
DIAG_MASK_PER_SLOPE = -1e32


def _nbytes(shape, dtype):
    return int(np.prod(shape)) * jnp.dtype(dtype).itemsize


def _params(semantics, blocks, extra_bytes=0):
    need = 2 * sum(_nbytes(s, d) for s, d in blocks) + extra_bytes + (4 << 20)
    return pltpu.CompilerParams(dimension_semantics=semantics,
                                vmem_limit_bytes=int(min(max(need, 16 << 20), VMEM_REQUEST_CAP)))


def _silu(v):
    u = 0.5 * v
    return u * jnp.tanh(u) + u


def _softplus(v):
    return jnp.maximum(v, 0.0) + jnp.log(1.0 + jnp.exp(-jnp.abs(v)))


class _CastRider(NamedTuple):
    weight: jax.Array
    layer: int
    block: Tuple[int, int]
    index: Callable


def _rider_specs(riders, n_trailing_index_args=0):
    in_specs, out_specs, out_shapes = [], [], []
    for rd in riders:
        def idx(*g, rd=rd, lead=()):
            g = g[:len(g) - n_trailing_index_args] if n_trailing_index_args else g
            return lead + tuple(rd.index(*g))
        in_specs.append(pl.BlockSpec((None,) + rd.block, functools.partial(idx, lead=(rd.layer,))))
        out_specs.append(pl.BlockSpec(rd.block, idx))
        out_shapes.append(jax.ShapeDtypeStruct(rd.weight.shape[1:], BF16))
    return in_specs, out_specs, out_shapes


def _rider_blocks(riders):
    return [(rd.block, F32) for rd in riders] + [(rd.block, BF16) for rd in riders]


def _with_riders(body, n_leading, n_out, n_riders):
    if not n_riders:
        return body

    def wrapped(*refs):
        rider_in = refs[n_leading:n_leading + n_riders]
        host_out = refs[n_leading + n_riders:n_leading + n_riders + n_out]
        rider_out = refs[n_leading + n_riders + n_out:n_leading + 2 * n_riders + n_out]
        for src, dst in zip(rider_in, rider_out):
            dst[...] = src[...].astype(dst.dtype)
        body(*refs[:n_leading], *host_out, *refs[n_leading + 2 * n_riders + n_out:])

    return wrapped


def _mod_kernel(c_ref, w_ref, b_ref, o_ref):
    @pl.when(pl.program_id(2) == 0)
    def _():
        o_ref[...] = jnp.broadcast_to(b_ref[...], o_ref.shape)

    c_act = _silu(c_ref[...]).astype(BF16)
    o_ref[...] += jnp.dot(c_act, w_ref[...].astype(BF16), preferred_element_type=F32)


def _modulation(c, w_mod, b_mod):
    depth, d, n = w_mod.shape
    rows = V7X_SUBLANES
    c_pad = jnp.zeros((rows, d), F32).at[:c.shape[0]].set(c)
    tk, tn = min(1024, d), min(2048, n)
    blocks = [((rows, tk), F32), ((tk, tn), F32), ((1, tn), F32), ((rows, tn), F32)]
    return pl.pallas_call(
        _mod_kernel,
        grid=(depth, n // tn, d // tk),
        in_specs=[pl.BlockSpec((rows, tk), lambda l, j, k: (0, k)),
                  pl.BlockSpec((None, tk, tn), lambda l, j, k: (l, k, j)),
                  pl.BlockSpec((None, 1, tn), lambda l, j, k: (l, 0, j))],
        out_specs=pl.BlockSpec((None, rows, tn), lambda l, j, k: (l, 0, j)),
        out_shape=jax.ShapeDtypeStruct((depth, rows, n), F32),
        name="adaln_mod",
        compiler_params=_params(("parallel", "parallel", "arbitrary"), blocks,
                                extra_bytes=_nbytes((tk, tn), BF16)),
    )(c_pad, w_mod, b_mod.reshape(depth, 1, n))


def _modulate_kernel(x_ref, scale_ref, shift_ref, h_ref):
    h_ref[...] = (x_ref[...] * (1.0 + scale_ref[...]) + shift_ref[...]).astype(h_ref.dtype)


def _modulate(x2, scale, shift, seq):
    m, d = x2.shape
    batch = m // seq
    ts = min(256, seq)
    n_s = seq // ts
    vec = pl.BlockSpec((None, 1, d), lambda b, i: (b, 0, 0))
    blocks = [((ts, d), F32), ((ts, d), BF16)]
    return pl.pallas_call(
        _modulate_kernel,
        grid=(batch, n_s),
        in_specs=[pl.BlockSpec((ts, d), lambda b, i: (b * n_s + i, 0)), vec, vec],
        out_specs=pl.BlockSpec((ts, d), lambda b, i: (b * n_s + i, 0)),
        out_shape=jax.ShapeDtypeStruct((m, d), BF16),
        name="modulate",
        compiler_params=_params(("parallel", "parallel"), blocks),
    )(x2, scale, shift)


def _matmul_kernel(a_ref, w_ref, o_ref):
    o_ref[...] = jnp.dot(a_ref[...], w_ref[...], preferred_element_type=F32).astype(o_ref.dtype)


def _matmul(a, w, *, tm, tn, out_dtype, name):
    m, k = a.shape
    n = w.shape[1]
    tm, tn = min(tm, m), min(tn, n)
    blocks = [((tm, k), a.dtype), ((k, tn), w.dtype), ((tm, tn), out_dtype)]
    return pl.pallas_call(
        _matmul_kernel,
        grid=(m // tm, n // tn),
        in_specs=[pl.BlockSpec((tm, k), lambda i, j: (i, 0)),
                  pl.BlockSpec((k, tn), lambda i, j: (0, j))],
        out_specs=pl.BlockSpec((tm, tn), lambda i, j: (i, j)),
        out_shape=jax.ShapeDtypeStruct((m, n), out_dtype),
        name=name,
        compiler_params=_params(("arbitrary", "arbitrary"), blocks, extra_bytes=_nbytes((tm, tn), F32)),
    )(a, w)


def _matmul_cat_kernel(a0_ref, a1_ref, w0_ref, w1_ref, o_ref):
    acc = jnp.dot(a0_ref[...], w0_ref[...], preferred_element_type=F32)
    acc += jnp.dot(a1_ref[...], w1_ref[...], preferred_element_type=F32)
    o_ref[...] = acc.astype(o_ref.dtype)


def _matmul_cat(a0, a1, w, *, tm, tn, out_dtype):
    m, k0 = a0.shape
    assert a1.shape == (m, k0) and w.shape[0] == 2 * k0
    n = w.shape[1]
    tm, tn = min(tm, m), min(tn, n)
    blocks = [((tm, 2 * k0), a0.dtype), ((2 * k0, tn), w.dtype), ((tm, tn), out_dtype)]
    return pl.pallas_call(
        _matmul_cat_kernel,
        grid=(m // tm, n // tn),
        in_specs=[pl.BlockSpec((tm, k0), lambda i, j: (i, 0)),
                  pl.BlockSpec((tm, k0), lambda i, j: (i, 0)),
                  pl.BlockSpec((k0, tn), lambda i, j: (0, j)),
                  pl.BlockSpec((k0, tn), lambda i, j: (1, j))],
        out_specs=pl.BlockSpec((tm, tn), lambda i, j: (i, j)),
        out_shape=jax.ShapeDtypeStruct((m, n), out_dtype),
        name="out_proj",
        compiler_params=_params(("parallel", "arbitrary"), blocks, extra_bytes=_nbytes((tm, tn), F32)),
    )(a0, a1, w, w)


def _in_proj_kernel(a_ref, w_ref, wdt_ref, o_ref, dt_ref, *, rc, j_act, j_act_end):
    j = pl.program_id(1)
    tm = o_ref.shape[0]

    @pl.when(j == 0)
    def _():
        dt_ref[...] = lax.dot_general(a_ref[...], wdt_ref[...], NT_DIMS, preferred_element_type=F32)

    activated = jnp.logical_and(j >= j_act, j < j_act_end)

    @pl.when(jnp.logical_not(activated))
    def _():
        o_ref[...] = lax.dot_general(a_ref[...], w_ref[...], NT_DIMS, preferred_element_type=F32).astype(o_ref.dtype)

    @pl.when(activated)
    def _():
        for r in range(tm // rc):
            rows = slice(r * rc, (r + 1) * rc)
            acc = lax.dot_general(a_ref[rows, :], w_ref[...], NT_DIMS, preferred_element_type=F32)
            o_ref[rows, :] = _silu(acc).astype(o_ref.dtype)


def _in_proj(h, w, w_dt, layer, *, seq, n_cols, act_col, act_end):
    m, k = h.shape
    tm, tn = min(1024, seq), 1024
    assert n_cols % tn == 0 and act_col % tn == 0 and act_end % tn == 0
    lanes = w_dt.shape[1]
    blocks = [((tm, k), BF16), ((k, tn), BF16), ((k, lanes), BF16), ((tm, tn), BF16), ((tm, lanes), F32)]
    return pl.pallas_call(
        functools.partial(_in_proj_kernel, rc=min(256, tm), j_act=act_col // tn, j_act_end=act_end // tn),
        grid=(m // tm, n_cols // tn),
        in_specs=[pl.BlockSpec((tm, k), lambda i, j: (i, 0)),
                  pl.BlockSpec((None, tn, k), lambda i, j: (layer, j, 0)),
                  pl.BlockSpec((None, lanes, k), lambda i, j: (layer, 0, 0))],
        out_specs=[pl.BlockSpec((tm, tn), lambda i, j: (i, j)),
                   pl.BlockSpec((tm, lanes), lambda i, j: (i, 0))],
        out_shape=[jax.ShapeDtypeStruct((m, n_cols), BF16), jax.ShapeDtypeStruct((m, lanes), F32)],
        name="in_proj",
        compiler_params=_params(("arbitrary", "arbitrary"), blocks, extra_bytes=3 * _nbytes((tm, tn), F32)),
    )(h, w, w_dt)


def _attn_kernel(slopes_ref, q_ref, k_ref, v_ref, diag_ref, lam_ref, nw_ref, o_ref, qs_ref, m_ref, l_ref,
                 acc_ref, *, tile, rb, rb_diag, lam_init):
    h = pl.program_id(1)
    i = pl.program_id(2)
    slope = slopes_ref[h] * LOG2_E
    dh = ATT_HEAD_DIM
    nt = (((1,), (1,)), ((), ()))
    for mp in range(2):
        q_mp = q_ref[:, mp * dh:(mp + 1) * dh].astype(F32) * (dh ** -0.5 * LOG2_E)
        qs_ref[mp] = q_mp.astype(BF16)

    lanes = V7X_LANES

    def lane_tiled(x, width):
        return jnp.concatenate([x] * (width // lanes), axis=1)

    def update(rows, mp, s, tile_bias, v_t, first):
        nrows, kw = s.shape
        smax = jnp.broadcast_to(jnp.max(s, axis=-1, keepdims=True), (nrows, lanes)) + tile_bias
        if first:
            m_new = smax
        else:
            m_old = m_ref[mp, rows]
            m_new = jnp.maximum(m_old, smax)
            alpha = jnp.exp2(m_old - m_new)
        p = jnp.exp2(s - lane_tiled(m_new - tile_bias, kw))
        psum = p[:, 0:lanes]
        for c in range(1, kw // lanes):
            psum = psum + p[:, c * lanes:(c + 1) * lanes]
        pv = jnp.dot(p.astype(BF16), v_t, preferred_element_type=F32)
        if first:
            l_ref[mp, rows] = psum
            acc_ref[mp, rows] = pv
        else:
            l_ref[mp, rows] = alpha * l_ref[mp, rows] + psum
            acc_ref[mp, rows] = lane_tiled(alpha, pv.shape[1]) * acc_ref[mp, rows] + pv
        m_ref[mp, rows] = m_new

    k_diag = pl.multiple_of(i * tile, tile)
    diag_bias = slope * (i * tile).astype(F32)
    for r in range(tile // rb_diag):
        rows = slice(r * rb_diag, (r + 1) * rb_diag)
        kw = (r + 1) * rb_diag
        bias = slope * diag_ref[rows, 0:kw]
        v_t = v_ref[pl.ds(k_diag, kw), :]
        for mp in range(2):
            kt = k_ref[pl.ds(k_diag, kw), mp * dh:(mp + 1) * dh]
            s = lax.dot_general(qs_ref[mp, rows], kt, nt, preferred_element_type=F32) + bias
            update(rows, mp, s, diag_bias, v_t, first=True)

    sweep = max(tile, ATT_SWEEP_KEYS)
    tiles_per_sweep = sweep // tile
    assert tiles_per_sweep in (1, 2)
    col_bias = slope * lax.broadcasted_iota(jnp.int32, (1, sweep), 1).astype(F32)

    def visible_keys(k0, kw):
        tile_bias = slope * k0.astype(F32)
        v_t = v_ref[pl.ds(k0, kw), :]
        for r in range(tile // rb):
            rows = slice(r * rb, (r + 1) * rb)
            for mp in range(2):
                kt = k_ref[pl.ds(k0, kw), mp * dh:(mp + 1) * dh]
                s = lax.dot_general(qs_ref[mp, rows], kt, nt, preferred_element_type=F32)
                update(rows, mp, s + col_bias[:, :kw], tile_bias, v_t, first=False)

    def body(jj, carry):
        visible_keys(pl.multiple_of(jj * sweep, sweep), sweep)
        return carry

    lax.fori_loop(0, i // tiles_per_sweep, body, 0)

    if tiles_per_sweep == 2:
        @pl.when(i % 2 == 1)
        def _():
            visible_keys(pl.multiple_of((i - 1) * tile, tile), tile)

    lp = lam_ref[...]
    lam = (jnp.exp(jnp.sum(lp[0:1] * lp[1:2], axis=-1, keepdims=True))
           - jnp.exp(jnp.sum(lp[2:3] * lp[3:4], axis=-1, keepdims=True)) + lam_init)
    inv_l = [1.0 / jnp.sum(l_ref[mp], axis=-1, keepdims=True) for mp in range(2)]
    o = acc_ref[0] * inv_l[0] - lam * (acc_ref[1] * inv_l[1])
    ms = jnp.mean(o * o, axis=-1, keepdims=True)
    y = o * lax.rsqrt(ms + RMS_EPS) * nw_ref[...] * (1.0 - lam_init)
    o_ref[...] = y.astype(o_ref.dtype)


def _attention(proj, lam_params, norm_w, *, seq, n_heads, lam_init, make_riders=None):
    m = proj.shape[0]
    batch = m // seq
    tile = min(ATT_TILE, seq)
    n_q = seq // tile
    dv = ATT_VALUE_DIM
    slopes = jnp.asarray(np.array([2.0 ** (-8.0 * (hh + 1) / n_heads) for hh in range(n_heads)],
                                  dtype=np.float32))
    tq = np.arange(tile)[:, None]
    tk = np.arange(tile)[None, :]
    diag_table = jnp.asarray(np.where(tk // CHUNK <= tq // CHUNK, tq - np.abs(tq - tk), DIAG_MASK_PER_SLOPE)
                             .astype(np.float32))
    riders = make_riders((batch, n_heads, n_q)) if make_riders else ()
    r_in, r_out, r_shapes = _rider_specs(riders, n_trailing_index_args=1)
    blocks = [((tile, dv), BF16), ((seq, dv), BF16), ((seq, dv), BF16), ((tile, tile), F32),
              ((tile, dv), BF16)] + _rider_blocks(riders)
    scratch_bytes = (2 * _nbytes((2, tile, V7X_LANES), F32) + _nbytes((2, tile, dv), F32)
                     + 6 * _nbytes((tile, tile), F32))
    grid_spec = pltpu.PrefetchScalarGridSpec(
        num_scalar_prefetch=1,
        grid=(batch, n_heads, n_q),
        in_specs=[pl.BlockSpec((tile, dv), lambda b, h, i, s: (b * n_q + i, h)),
                  pl.BlockSpec((seq, dv), lambda b, h, i, s: (b, n_heads + h)),
                  pl.BlockSpec((seq, dv), lambda b, h, i, s: (b, 2 * n_heads + h)),
                  pl.BlockSpec((tile, tile), lambda b, h, i, s: (0, 0)),
                  pl.BlockSpec((4, ATT_HEAD_DIM), lambda b, h, i, s: (0, 0)),
                  pl.BlockSpec((1, dv), lambda b, h, i, s: (0, 0))] + r_in,
        out_specs=[pl.BlockSpec((tile, dv), lambda b, h, i, s: (b * n_q + i, h))] + r_out,
        scratch_shapes=[pltpu.VMEM((2, tile, ATT_HEAD_DIM), BF16), pltpu.VMEM((2, tile, V7X_LANES), F32),
                        pltpu.VMEM((2, tile, V7X_LANES), F32), pltpu.VMEM((2, tile, dv), F32)],
    )
    body = functools.partial(_attn_kernel, tile=tile, rb=min(ATT_ROW_BLOCK, tile),
                             rb_diag=min(ATT_DIAG_ROW_BLOCK, tile), lam_init=lam_init)
    out = pl.pallas_call(
        _with_riders(body, 7, 1, len(riders)),
        grid_spec=grid_spec,
        out_shape=[jax.ShapeDtypeStruct((m, n_heads * dv), BF16)] + r_shapes,
        name="diff_attention",
        compiler_params=_params(("arbitrary", "arbitrary", "arbitrary"), blocks, extra_bytes=scratch_bytes),
    )(slopes, proj, proj, proj, diag_table, lam_params, norm_w.reshape(1, dv), *[rd.weight for rd in riders])
    return (out[0], out[1:]) if riders else out[0]


def _ssd_kernel(z_ref, xbc_ref, halo_ref, dt_ref, dtt_ref, cw_ref, cb_ref, dtb_l_ref, alog_l_ref,
                dtb_s_ref, alog_s_ref, d_ref, nw_ref, o_ref, state_ref, xd_ref, *, tile, width):
    t = pl.program_id(1)
    gw = width // SSD_GROUPS
    hpg = gw // SSD_HEAD_DIM
    n = SSD_STATE
    b_off, c_off = width, width + SSD_GROUPS * n

    @pl.when(t == 0)
    def _():
        state_ref[...] = jnp.zeros_like(state_ref)

    row = lax.broadcasted_iota(jnp.int32, (tile, tile), 0)
    col = lax.broadcasted_iota(jnp.int32, (tile, tile), 1)
    causal = row >= col
    dt_s = _softplus(dt_ref[...] + dtb_l_ref[...])
    da_s = dt_s * (-jnp.exp(alog_l_ref[...]))
    acs_s = jnp.dot(causal.astype(F32), da_s, precision=lax.Precision.HIGHEST,
                    preferred_element_type=F32)
    dt_l = _softplus(dtt_ref[...] + dtb_s_ref[...])
    da_l = dt_l * (-jnp.exp(alog_s_ref[...]))
    acs_l = jnp.dot(da_l, (row <= col).astype(F32), precision=lax.Precision.HIGHEST,
                    preferred_element_type=F32)

    head_of_lane = lax.broadcasted_iota(jnp.int32, (1, gw), 1) // SSD_HEAD_DIM

    def per_head(vals):
        out = vals[hpg - 1]
        for r in range(hpg - 2, -1, -1):
            out = jnp.where(head_of_lane == r, vals[r], out)
        return out

    def lane_tiled(x, w):
        return jnp.concatenate([x] * (w // V7X_LANES), axis=1)

    shifts = jnp.concatenate([(row - col == k).astype(BF16) for k in range(1, SSD_CONV)], axis=0)
    xd_ref[...] = jnp.dot(shifts, xbc_ref[...], preferred_element_type=F32)
    sub = V7X_SUBLANES
    before = jnp.where(t > 0, halo_ref[halo_ref.shape[0] - sub:, :].astype(F32), 0.0)
    head = jnp.concatenate([before, xbc_ref[0:sub, :].astype(F32)], axis=0)
    for k in range(1, SSD_CONV):
        xd_ref[(k - 1) * tile:(k - 1) * tile + sub, :] = head[sub - k:2 * sub - k]

    def conv_silu(c0, w):
        lanes = slice(c0, c0 + w)
        y = cb_ref[:, lanes] + cw_ref[SSD_CONV - 1:SSD_CONV, lanes] * xbc_ref[:, lanes].astype(F32)
        for k in range(1, SSD_CONV):
            y = y + cw_ref[SSD_CONV - 1 - k:SSD_CONV - k, lanes] * xd_ref[(k - 1) * tile:k * tile, lanes]
        return _silu(y)

    for g in range(SSD_GROUPS):
        xs = conv_silu(g * gw, gw)
        bm = conv_silu(b_off + g * n, n).astype(BF16)
        cm = conv_silu(c_off + g * n, n).astype(BF16)
        heads = [g * hpg + r for r in range(hpg)]
        acs_rep = [jnp.broadcast_to(acs_s[:, hh:hh + 1], (tile, V7X_LANES)) for hh in heads]
        dt_rep = [jnp.broadcast_to(dt_s[:, hh:hh + 1], (tile, V7X_LANES)) for hh in heads]
        acs_sel = per_head([lane_tiled(a, gw) for a in acs_rep])
        last_sel = acs_sel[tile - 1:tile, :]
        xdt = xs * per_head([lane_tiled(d, gw) for d in dt_rep])
        cb = lax.dot_general(cm, bm, (((1,), (1,)), ((), ())), preferred_element_type=F32)
        prev = state_ref[g]
        y = jnp.dot(cm, prev.astype(BF16), preferred_element_type=F32) * jnp.exp(acs_sel)
        hb = tile // 2
        y_top, y_bot = y[:hb], y[hb:]
        for r, hh in enumerate(heads):
            x_r = jnp.where(head_of_lane == r, xdt, 0.0).astype(BF16)
            seg_top = lane_tiled(acs_rep[r][:hb], hb) - acs_l[hh:hh + 1, :hb]
            g_top = cb[:hb, :hb] * jnp.exp(jnp.where(causal[:hb, :hb], seg_top, MASK_VALUE))
            y_top = y_top + jnp.dot(g_top.astype(BF16), x_r[:hb], preferred_element_type=F32)
            seg_bot = lane_tiled(acs_rep[r][hb:], tile) - acs_l[hh:hh + 1, :]
            g_bot = cb[hb:, :] * jnp.exp(jnp.where(causal[hb:, :], seg_bot, MASK_VALUE))
            y_bot = y_bot + jnp.dot(g_bot.astype(BF16), x_r, preferred_element_type=F32)
        y = jnp.concatenate([y_top, y_bot], axis=0)
        to_end = jnp.exp(last_sel - acs_sel)
        new_state = lax.dot_general(bm, (xdt * to_end).astype(BF16), (((0,), (0,)), ((), ())),
                                    preferred_element_type=F32)
        state_ref[g] = prev * jnp.exp(last_sel) + new_state
        lanes = slice(g * gw, (g + 1) * gw)
        y = y + d_ref[:, lanes] * xs
        y = y * z_ref[:, lanes].astype(F32)
        y = y * lax.rsqrt(jnp.mean(y * y, axis=-1, keepdims=True) + RMS_EPS) * nw_ref[:, lanes]
        o_ref[:, lanes] = y.astype(o_ref.dtype)


def _ssd(proj, dt_raw, conv_w, conv_b, dt_bias, a_log, d_skip, norm_w, *, seq, z_col, xbc_col, make_riders=None):
    m = proj.shape[0]
    batch = m // seq
    n_heads = dt_bias.shape[0]
    width = n_heads * SSD_HEAD_DIM
    conv_dim = conv_w.shape[1]
    tile = min(256, seq)
    n_t = seq // tile
    halo = V7X_BF16_SUBLANES
    halo_per_tile = tile // halo
    lanes = dt_raw.shape[1]

    def lane_vec(v):
        return jnp.zeros((1, lanes), F32).at[0, :n_heads].set(v)

    dtt = dt_raw[:, :n_heads].T
    const = lambda b, t: (0, 0)
    riders = make_riders((batch, n_t)) if make_riders else ()
    r_in, r_out, r_shapes = _rider_specs(riders)
    blocks = [((tile, width), BF16), ((tile, conv_dim), BF16), ((halo, conv_dim), BF16),
              ((tile, lanes), F32), ((n_heads, tile), F32), ((tile, width), BF16)] + _rider_blocks(riders)
    scratch_bytes = _nbytes((SSD_GROUPS, SSD_STATE, width // SSD_GROUPS), F32) + 12 * _nbytes((tile, tile), F32) \
        + _nbytes(((SSD_CONV - 1) * tile, conv_dim), F32)
    out = pl.pallas_call(
        _with_riders(functools.partial(_ssd_kernel, tile=tile, width=width), 13, 1, len(riders)),
        grid=(batch, n_t),
        in_specs=[pl.BlockSpec((tile, width), lambda b, t: (b * n_t + t, z_col)),
                  pl.BlockSpec((tile, conv_dim), lambda b, t: (b * n_t + t, xbc_col)),
                  pl.BlockSpec((halo, conv_dim),
                               lambda b, t: (jnp.maximum((b * n_t + t) * halo_per_tile - 1, 0), xbc_col)),
                  pl.BlockSpec((tile, lanes), lambda b, t: (b * n_t + t, 0)),
                  pl.BlockSpec((n_heads, tile), lambda b, t: (0, b * n_t + t)),
                  pl.BlockSpec((SSD_CONV, conv_dim), const),
                  pl.BlockSpec((1, conv_dim), const),
                  pl.BlockSpec((1, lanes), const),
                  pl.BlockSpec((1, lanes), const),
                  pl.BlockSpec((n_heads, 1), const),
                  pl.BlockSpec((n_heads, 1), const),
                  pl.BlockSpec((1, width), const),
                  pl.BlockSpec((1, width), const)] + r_in,
        out_specs=[pl.BlockSpec((tile, width), lambda b, t: (b * n_t + t, 0))] + r_out,
        out_shape=[jax.ShapeDtypeStruct((m, width), BF16)] + r_shapes,
        name="ssd_mixer",
        scratch_shapes=[pltpu.VMEM((SSD_GROUPS, SSD_STATE, width // SSD_GROUPS), F32),
                        pltpu.VMEM(((SSD_CONV - 1) * tile, conv_dim), F32)],
        compiler_params=_params(("arbitrary", "arbitrary"), blocks, extra_bytes=scratch_bytes),
    )(proj, proj, proj, dt_raw, dtt, conv_w, conv_b.reshape(1, conv_dim), lane_vec(dt_bias), lane_vec(a_log),
      dt_bias.reshape(n_heads, 1), a_log.reshape(n_heads, 1),
      jnp.repeat(d_skip, SSD_HEAD_DIM).reshape(1, width), norm_w.reshape(1, width),
      *[rd.weight for rd in riders])
    return (out[0], out[1:]) if riders else out[0]


def _ln_kernel(*refs, alpha, emit_next):
    if emit_next:
        x_ref, br_ref, gate_ref, g_ref, b_ref, scale_ref, shift_ref, xo_ref, h_ref = refs
    else:
        x_ref, br_ref, gate_ref, g_ref, b_ref, xo_ref = refs
    v = alpha * x_ref[...] + gate_ref[...] * br_ref[...].astype(F32)
    mu = jnp.mean(v, axis=-1, keepdims=True)
    dev = v - mu
    var = jnp.mean(dev * dev, axis=-1, keepdims=True)
    y = dev * lax.rsqrt(var + LN_EPS) * g_ref[...] + b_ref[...]
    xo_ref[...] = y
    if emit_next:
        h_ref[...] = (y * (1.0 + scale_ref[...]) + shift_ref[...]).astype(h_ref.dtype)


def _residual_ln(x2, branch, gate, ln_g, ln_b, next_scale, next_shift, *, seq, alpha):
    m, d = x2.shape
    batch = m // seq
    ts = min(256, seq)
    n_s = seq // ts
    emit_next = next_scale is not None
    rows = pl.BlockSpec((ts, d), lambda b, i: (b * n_s + i, 0))
    per_batch = pl.BlockSpec((None, 1, d), lambda b, i: (b, 0, 0))
    shared = pl.BlockSpec((1, d), lambda b, i: (0, 0))
    in_specs = [rows, rows, per_batch, shared, shared]
    args = [x2, branch, gate, ln_g.reshape(1, d), ln_b.reshape(1, d)]
    out_specs = [rows]
    out_shape = [jax.ShapeDtypeStruct((m, d), F32)]
    blocks = [((ts, d), F32), ((ts, d), branch.dtype), ((ts, d), F32)]
    if emit_next:
        in_specs += [per_batch, per_batch]
        args += [next_scale, next_shift]
        out_specs.append(rows)
        out_shape.append(jax.ShapeDtypeStruct((m, d), BF16))
        blocks.append(((ts, d), BF16))
    out = pl.pallas_call(
        functools.partial(_ln_kernel, alpha=alpha, emit_next=emit_next),
        grid=(batch, n_s),
        in_specs=in_specs,
        out_specs=out_specs,
        out_shape=out_shape,
        name="residual_ln",
        compiler_params=_params(("parallel", "parallel"), blocks, extra_bytes=2 * _nbytes((ts, d), F32)),
    )(*args)
    return (out[0], out[1]) if emit_next else (out[0], None)


def _ffn_up_kernel(h_ref, wg_ref, wu_ref, cw_ref, cb_ref, o_ref, carry_ref, *, tm, rc, tiles_per_seq):
    i = pl.program_id(0)
    j = pl.program_id(1)
    tn = o_ref.shape[1]
    tail = jnp.where(i % tiles_per_seq == 0, 0.0, carry_ref[j])
    ridx = lax.broadcasted_iota(jnp.int32, (rc, tn), 0)
    for r in range(tm // rc):
        rows = slice(r * rc, (r + 1) * rc)
        gate = jnp.dot(h_ref[rows, :], wg_ref[...], preferred_element_type=F32)
        up = jnp.dot(h_ref[rows, :], wu_ref[...], preferred_element_type=F32)
        p1 = tail[V7X_SUBLANES - 1:V7X_SUBLANES]
        p2 = tail[V7X_SUBLANES - 2:V7X_SUBLANES - 1]
        g1 = jnp.where(ridx == 0, p1, pltpu.roll(gate, 1, 0))
        g2 = jnp.where(ridx == 0, p2, jnp.where(ridx == 1, p1, pltpu.roll(gate, 2, 0)))
        y = cb_ref[...] + cw_ref[0:1, :] * g2 + cw_ref[1:2, :] * g1 + cw_ref[2:3, :] * gate
        o_ref[rows, :] = (_silu(y) * up).astype(o_ref.dtype)
        tail = gate[rc - V7X_SUBLANES:, :]
    carry_ref[j] = tail


def _ffn_up(h, w_gate, w_up, conv_w, conv_b, *, seq, tn):
    m, d = h.shape
    f = w_gate.shape[1]
    tm = min(2048, seq)
    rc = min(512, tm)
    n_j = f // tn
    blocks = [((tm, d), BF16), ((d, tn), BF16), ((d, tn), BF16), ((tm, tn), BF16)]
    w_spec = pl.BlockSpec((d, tn), lambda i, j: (0, j))
    return pl.pallas_call(
        functools.partial(_ffn_up_kernel, tm=tm, rc=rc, tiles_per_seq=seq // tm),
        grid=(m // tm, n_j),
        in_specs=[pl.BlockSpec((tm, d), lambda i, j: (i, 0)), w_spec, w_spec,
                  pl.BlockSpec((FFN_CONV, tn), lambda i, j: (0, j)),
                  pl.BlockSpec((1, tn), lambda i, j: (0, j))],
        out_specs=pl.BlockSpec((tm, tn), lambda i, j: (i, j)),
        out_shape=jax.ShapeDtypeStruct((m, f), BF16),
        name="ffn_up",
        scratch_shapes=[pltpu.VMEM((n_j, V7X_SUBLANES, tn), F32)],
        compiler_params=_params(("arbitrary", "arbitrary"), blocks,
                                extra_bytes=8 * _nbytes((rc, tn), F32)),
    )(h, w_gate, w_up, conv_w, conv_b.reshape(1, f))


def kernel(x, c, w_mod, b_mod, w_in, diff_lambda, diff_norm_w, ssd_conv_w, ssd_conv_b, ssd_dt_bias,
           ssd_a_log, ssd_d, ssd_norm_w, w_out, ln1_g, ln1_b, w_gate, w_up, ffn_conv_w, ffn_conv_b,
           w_down, ln2_g, ln2_b):
    batch, seq, d = x.shape
    depth = w_mod.shape[0]
    m = batch * seq
    n_ssd_heads = ssd_dt_bias.shape[1]
    ssd_width = n_ssd_heads * SSD_HEAD_DIM
    conv_dim = ssd_conv_w.shape[2]
    n_in = w_in.shape[2]
    att_width = (n_in - n_ssd_heads - conv_dim - ssd_width) // 3
    n_att_heads = att_width // ATT_VALUE_DIM
    n_main = n_in - n_ssd_heads
    assert att_width == ssd_width and conv_dim == 2 * ssd_width
    alpha = (2.0 * depth) ** 0.25
    ffn_tn = 256

    mod = _modulation(c, w_mod, b_mod)[:, :batch]
    mod = mod.reshape(depth, batch, 6, 1, d)
    shift1, scale1, gate1, shift2, scale2, gate2 = (mod[:, :, s] for s in range(6))

    def row_split(weight, layer):
        def make(grid):
            steps = int(np.prod(grid))
            rows, cols = weight.shape[1:]
            assert rows % (steps * V7X_BF16_SUBLANES) == 0

            def index(*ids):
                flat = ids[0]
                for extent, idx in zip(grid[1:], ids[1:]):
                    flat = flat * extent + idx
                return flat, 0
            return _CastRider(weight, layer, (rows // steps, cols), index)
        return make

    def tile_split(weight, layer):
        def make(grid):
            n_b, n_t = grid
            rows, cols = weight.shape[1:]
            assert rows % (n_t * V7X_BF16_SUBLANES) == 0 and cols % (n_b * V7X_LANES) == 0
            return _CastRider(weight, layer, (rows // n_t, cols // n_b), lambda b, t: (t, b))
        return make

    x2 = x.reshape(m, d)
    h = _modulate(x2, scale1[0], shift1[0], seq)
    w_in_t = jnp.swapaxes(w_in, 1, 2).astype(BF16)
    w_dt = jnp.pad(w_in_t[:, n_main:, :], ((0, 0), (0, V7X_LANES - n_ssd_heads), (0, 0)))
    for l in range(depth):
        last = l == depth - 1
        proj, dt_raw = _in_proj(h, w_in_t, w_dt, l, seq=seq, n_cols=n_main,
                                act_col=3 * att_width, act_end=3 * att_width + ssd_width)
        lam_init = 0.8 - 0.6 * float(np.exp(-0.3 * l))
        y_att, (w_gate_l, w_up_l) = _attention(
            proj, diff_lambda[l], diff_norm_w[l], seq=seq, n_heads=n_att_heads, lam_init=lam_init,
            make_riders=lambda grid: [row_split(w_gate, l)(grid), row_split(w_up, l)(grid)])
        y_ssd, (w_down_l, w_out_l) = _ssd(
            proj, dt_raw, ssd_conv_w[l], ssd_conv_b[l], ssd_dt_bias[l], ssd_a_log[l], ssd_d[l],
            ssd_norm_w[l], seq=seq, z_col=3 * att_width // ssd_width,
            xbc_col=(3 * att_width + ssd_width) // conv_dim,
            make_riders=lambda grid: [tile_split(w_down, l)(grid), tile_split(w_out, l)(grid)])
        mix = _matmul_cat(y_att, y_ssd, w_out_l, tm=1024, tn=1024, out_dtype=BF16)
        x2, h = _residual_ln(x2, mix, gate1[l], ln1_g[l], ln1_b[l], scale2[l], shift2[l],
                             seq=seq, alpha=alpha)
        act = _ffn_up(h, w_gate_l, w_up_l, ffn_conv_w[l], ffn_conv_b[l], seq=seq, tn=ffn_tn)
        ffn = _matmul(act, w_down_l, tm=512, tn=512, out_dtype=BF16, name="ffn_down")
        x2, h = _residual_ln(x2, ffn, gate2[l], ln2_g[l], ln2_b[l],
                             None if last else scale1[l + 1], None if last else shift1[l + 1],
                             seq=seq, alpha=alpha)
    return x2.reshape(batch, seq, d)
```
